```python
import math
import jax, jax.numpy as jnp
from jax import lax
import numpy as np

D_MODEL = 2048
BATCH = 1
SEQ = 16384
DEPTH = 2

N_A_LAYERS = DEPTH // 2
N_B_LAYERS = DEPTH - N_A_LAYERS
RMS_EPS = 1e-6
L2_EPS = 1e-6
NEG_INF = -1e30
FORCE_SCORE = 1e4

GDN_QK_HEADS = 16
GDN_V_HEADS = 32
GDN_HEAD_DIM = 128
GDN_QK_DIM = GDN_QK_HEADS * GDN_HEAD_DIM
GDN_V_DIM = GDN_V_HEADS * GDN_HEAD_DIM
GDN_CONV_DIM = 2 * GDN_QK_DIM + GDN_V_DIM
GDN_CONV = 4
GDN_CHUNK = 64
GDN_IN_DIM = GDN_CONV_DIM + GDN_V_DIM + 2 * GDN_V_HEADS

NSA_HEADS = 16
NSA_KV_GROUPS = 4
NSA_HEADS_PER_GROUP = NSA_HEADS // NSA_KV_GROUPS
NSA_HEAD_DIM = 128
NSA_Q_DIM = NSA_HEADS * NSA_HEAD_DIM
NSA_IN_DIM = NSA_Q_DIM + 3 * NSA_HEADS
NSA_KV_DIM = 6 * NSA_KV_GROUPS * NSA_HEAD_DIM
CMP_BLOCK = 32
CMP_STRIDE = 16
CMP_HIDDEN = 256
SEL_BLOCK = 64
SEL_COUNT = 16
WINDOW = 512
Q_BLOCK = 128
SEL_RATIO = SEL_BLOCK // CMP_STRIDE
SEL_OVERLAP_W = (1.0, 2.0, 2.0, 2.0, 1.0)

FFN_HIDDEN = -(-8 * D_MODEL // (3 * 256)) * 256

kernel_name = "yoco_gdn_nsa_hybrid"


def rms_norm(x, g):
    xf = x.astype(jnp.float32)
    y = xf * lax.rsqrt(jnp.mean(xf * xf, axis=-1, keepdims=True) + RMS_EPS)
    return (y * g.astype(jnp.float32)).astype(x.dtype)


def l2_normalize(x):
    xf = x.astype(jnp.float32)
    return xf * lax.rsqrt(jnp.sum(xf * xf, axis=-1, keepdims=True) + L2_EPS)


def alibi_slopes(n):
    return 2.0 ** (-8.0 * jnp.arange(1, n + 1, dtype=jnp.float32) / n)


def swiglu_ffn(x, w_gate_up, w_down):
    gate, up = jnp.split(x @ w_gate_up, 2, axis=-1)
    return (jax.nn.silu(gate) * up) @ w_down


def causal_depthwise_conv(x, w):
    c = x.shape[-1]
    return lax.conv_general_dilated(
        x, w[:, None, :], window_strides=(1,), padding=[(w.shape[0] - 1, 0)],
        dimension_numbers=('NWC', 'WIO', 'NWC'), feature_group_count=c)


def chunk_gated_delta_rule(q, k, v, g, beta):
    B, T, H, Dk = q.shape
    Dv = v.shape[-1]
    C = GDN_CHUNK
    nc = T // C

    def to_chunks(a):
        a = jnp.moveaxis(a, 2, 1)
        return a.reshape(B, H, nc, C, *a.shape[3:])

    q = to_chunks(q) * (Dk ** -0.5)
    k, v, g, beta = to_chunks(k), to_chunks(v), to_chunks(g), to_chunks(beta)
    g_cum = jnp.cumsum(g, axis=-1)
    causal = jnp.tril(jnp.ones((C, C), dtype=bool))
    strict = jnp.tril(jnp.ones((C, C), dtype=bool), -1)
    decay = jnp.exp(jnp.where(causal, g_cum[..., :, None] - g_cum[..., None, :], -jnp.inf))
    k_beta = k * beta[..., None]
    l_mat = jnp.where(strict, jnp.einsum('bhnid,bhnjd->bhnij', k_beta, k) * decay, 0.0)
    rhs = jnp.concatenate([v * beta[..., None], k_beta * jnp.exp(g_cum)[..., None]], axis=-1)
    sol = lax.linalg.triangular_solve(l_mat, rhs, left_side=True, lower=True, unit_diagonal=True)
    u, w = sol[..., :Dv], sol[..., Dv:]
    attn_qk = jnp.einsum('bhnid,bhnjd->bhnij', q, k) * decay
    q_dec = q * jnp.exp(g_cum)[..., None]
    g_last = g_cum[..., -1]
    k_dec = k * jnp.exp(g_last[..., None] - g_cum)[..., None]
    xs = (jnp.moveaxis(q_dec, 2, 0), jnp.moveaxis(attn_qk, 2, 0), jnp.moveaxis(u, 2, 0),
          jnp.moveaxis(w, 2, 0), jnp.moveaxis(k_dec, 2, 0), jnp.moveaxis(g_last, 2, 0))

    def step(state, inp):
        q_c, a_c, u_c, w_c, k_c, gl = inp
        v_new = u_c - jnp.einsum('bhck,bhkv->bhcv', w_c, state)
        o_c = jnp.einsum('bhck,bhkv->bhcv', q_c, state) + jnp.einsum('bhij,bhjv->bhiv', a_c, v_new)
        state = state * jnp.exp(gl)[..., None, None] + jnp.einsum('bhck,bhcv->bhkv', k_c, v_new)
        return state, o_c

    state0 = jnp.zeros((B, H, Dk, Dv), jnp.float32)
    _, o = lax.scan(step, state0, xs)
    o = jnp.moveaxis(o, 0, 2).reshape(B, H, T, Dv)
    return jnp.moveaxis(o, 1, 2)


def gated_deltanet(x, w_in, conv_w, a_log, dt_bias, out_norm_g, w_out):
    B, T, _ = x.shape
    f32 = jnp.float32
    proj = x @ w_in
    qkv, z, b_raw, a_raw = jnp.split(
        proj, [GDN_CONV_DIM, GDN_CONV_DIM + GDN_V_DIM, GDN_CONV_DIM + GDN_V_DIM + GDN_V_HEADS], axis=-1)
    qkv = jax.nn.silu(causal_depthwise_conv(qkv, conv_w))
    q, k, v = jnp.split(qkv, [GDN_QK_DIM, 2 * GDN_QK_DIM], axis=-1)
    rep = GDN_V_HEADS // GDN_QK_HEADS
    q = jnp.repeat(l2_normalize(q.reshape(B, T, GDN_QK_HEADS, GDN_HEAD_DIM)), rep, axis=2)
    k = jnp.repeat(l2_normalize(k.reshape(B, T, GDN_QK_HEADS, GDN_HEAD_DIM)), rep, axis=2)
    v = v.reshape(B, T, GDN_V_HEADS, GDN_HEAD_DIM).astype(f32)
    beta = jax.nn.sigmoid(b_raw.astype(f32))
    g = -jnp.exp(a_log.astype(f32)) * jax.nn.softplus(a_raw.astype(f32) + dt_bias.astype(f32))
    o = chunk_gated_delta_rule(q, k, v, g, beta)
    o = rms_norm(o, out_norm_g) * jax.nn.silu(z.reshape(B, T, GDN_V_HEADS, GDN_HEAD_DIM).astype(f32))
    return o.reshape(B, T, GDN_V_DIM).astype(x.dtype) @ w_out


def compress_blocks(x_tok, pos, w1, w2):
    B, G, T, D = x_tok.shape
    seg = x_tok.reshape(B, G, T // CMP_STRIDE, CMP_STRIDE, D)
    blocks = jnp.concatenate([seg[:, :, :-1], seg[:, :, 1:]], axis=3)
    blocks = (blocks + pos).reshape(B, G, blocks.shape[2], CMP_BLOCK * D)
    return jax.nn.silu(blocks @ w1) @ w2


def nsa_shared_kv(h, kv_norm_g, kv_w, cmp_pos, cmp_w1_k, cmp_w2_k, cmp_w1_v, cmp_w2_v):
    B, T, _ = h.shape
    kv = (rms_norm(h, kv_norm_g) @ kv_w).reshape(B, T, 6, NSA_KV_GROUPS, NSA_HEAD_DIM)
    kv = jnp.transpose(kv, (2, 0, 3, 1, 4))
    k_cmp = compress_blocks(kv[0], cmp_pos, cmp_w1_k, cmp_w2_k)
    v_cmp = compress_blocks(kv[1], cmp_pos, cmp_w1_v, cmp_w2_v)
    nsb = T // SEL_BLOCK
    k_slc = kv[2].reshape(B, NSA_KV_GROUPS, nsb, SEL_BLOCK, NSA_HEAD_DIM)
    v_slc = kv[3].reshape(B, NSA_KV_GROUPS, nsb, SEL_BLOCK, NSA_HEAD_DIM)
    pad = ((0, 0), (0, 0), (WINDOW, 0), (0, 0))
    k_win = jnp.pad(kv[4], pad)
    v_win = jnp.pad(kv[5], pad)
    return (k_cmp, v_cmp, k_slc, v_slc, k_win, v_win)


def nsa_attention(x, shared_kv, w_in, w_out):
    k_cmp, v_cmp, k_slc, v_slc, k_win, v_win = shared_kv
    B, T, _ = x.shape
    G, HG, D = NSA_KV_GROUPS, NSA_HEADS_PER_GROUP, NSA_HEAD_DIM
    f32 = jnp.float32
    nqb = T // Q_BLOCK
    ncb = k_cmp.shape[2]
    nsb = k_slc.shape[2]
    n_sel = min(SEL_COUNT, nsb)
    proj = x @ w_in
    q = proj[..., :NSA_Q_DIM] * (D ** -0.5)
    q = jnp.transpose(q.reshape(B, nqb, Q_BLOCK, G, HG, D), (1, 0, 3, 4, 2, 5))
    gates = jax.nn.sigmoid(proj[..., NSA_Q_DIM:].astype(f32))
    gates = jnp.transpose(gates.reshape(B, nqb, Q_BLOCK, 3, G, HG), (1, 3, 0, 4, 5, 2))
    starts = jnp.arange(nqb, dtype=jnp.int32) * Q_BLOCK

    slopes = alibi_slopes(NSA_HEADS).reshape(1, G, HG, 1, 1)
    cmp_idx = jnp.arange(ncb, dtype=jnp.int32)
    cmp_end = cmp_idx * CMP_STRIDE + (CMP_BLOCK - 1)
    cmp_centre = cmp_idx.astype(f32) * CMP_STRIDE + (CMP_BLOCK - 1) / 2
    blk = jnp.arange(nsb, dtype=jnp.int32)
    b_ix = jnp.arange(B)[:, None, None, None]
    g_ix = jnp.arange(G)[None, :, None, None]

    def query_block(args):
        qb, gb, start = args
        t = start + jnp.arange(Q_BLOCK, dtype=jnp.int32)
        tf = t.astype(f32)
        valid_c = cmp_end[None, :] <= t[:, None]
        s_c = jnp.einsum('bghqd,bgnd->bghqn', qb, k_cmp, preferred_element_type=f32)
        s_c = s_c - slopes * (tf[:, None] - cmp_centre[None, :])
        p_c = jnp.where(valid_c, jax.nn.softmax(jnp.where(valid_c, s_c, NEG_INF), axis=-1), 0.0)
        o_cmp = jnp.einsum('bghqn,bgnd->bghqd', p_c.astype(v_cmp.dtype), v_cmp)
        imp = jnp.pad(p_c.sum(axis=2), ((0, 0), (0, 0), (0, 0), (1, SEL_RATIO * nsb + SEL_RATIO - 1 - ncb)))
        imp_slc = sum(wt * imp[..., o:o + SEL_RATIO * nsb:SEL_RATIO] for o, wt in enumerate(SEL_OVERLAP_W))
        cur = t // SEL_BLOCK
        forced = (blk[None, :] == 0) | (blk[None, :] == cur[:, None]) | (blk[None, :] == cur[:, None] - 1)
        visible = blk[None, :] * SEL_BLOCK <= t[:, None]
        score = jnp.where(visible, jnp.where(forced, FORCE_SCORE, imp_slc), NEG_INF)
        _, sel = lax.top_k(score, n_sel)
        k_sel = k_slc[b_ix, g_ix, sel]
        v_sel = v_slc[b_ix, g_ix, sel]
        dist_s = t[:, None, None] - (sel[..., None] * SEL_BLOCK + jnp.arange(SEL_BLOCK, dtype=jnp.int32))
        dist_s = dist_s[:, :, None]
        s_s = jnp.einsum('bghqd,bgqnrd->bghqnr', qb, k_sel, preferred_element_type=f32)
        s_s = jnp.where(dist_s >= 0, s_s - slopes[..., None] * dist_s.astype(f32), NEG_INF)
        p_s = jax.nn.softmax(s_s.reshape(B, G, HG, Q_BLOCK, n_sel * SEL_BLOCK), axis=-1).reshape(s_s.shape)
        o_slc = jnp.einsum('bghqnr,bgqnrd->bghqd', p_s.astype(v_sel.dtype), v_sel)
        k_w = lax.dynamic_slice_in_dim(k_win, start, Q_BLOCK + WINDOW, axis=2)
        v_w = lax.dynamic_slice_in_dim(v_win, start, Q_BLOCK + WINDOW, axis=2)
        kpos = start - WINDOW + jnp.arange(Q_BLOCK + WINDOW, dtype=jnp.int32)
        dist_w = t[:, None] - kpos[None, :]
        valid_w = (dist_w >= 0) & (dist_w < WINDOW) & (kpos[None, :] >= 0)
        s_w = jnp.einsum('bghqd,bgkd->bghqk', qb, k_w, preferred_element_type=f32) - slopes * dist_w.astype(f32)
        p_w = jax.nn.softmax(jnp.where(valid_w, s_w, NEG_INF), axis=-1)
        o_win = jnp.einsum('bghqk,bgkd->bghqd', p_w.astype(v_w.dtype), v_w)
        return gb[0][..., None] * o_cmp + gb[1][..., None] * o_slc + gb[2][..., None] * o_win

    o = lax.map(query_block, (q, gates, starts))
    o = jnp.transpose(o, (1, 0, 4, 2, 3, 5)).reshape(B, T, NSA_Q_DIM)
    return o.astype(x.dtype) @ w_out


def setup_inputs(seed: int = 0) -> dict:
    key = jax.random.key(seed)
    ks = jax.random.split(key, 24)
    f32 = jnp.float32

    def dense(k, shape, fan_in):
        return jax.random.normal(k, shape, f32) * fan_in ** -0.5

    def gain(k, shape):
        return 1.0 + 0.02 * jax.random.normal(k, shape, f32)

    x = jax.random.normal(ks[0], (BATCH, SEQ, D_MODEL), f32)
    mixer_norm_g = gain(ks[1], (DEPTH, D_MODEL))
    ffn_norm_g = gain(ks[2], (DEPTH, D_MODEL))
    ffn_w_gate_up = dense(ks[3], (DEPTH, D_MODEL, 2 * FFN_HIDDEN), D_MODEL)
    ffn_w_down = dense(ks[4], (DEPTH, FFN_HIDDEN, D_MODEL), FFN_HIDDEN)
    gdn_w_in = dense(ks[5], (N_A_LAYERS, D_MODEL, GDN_IN_DIM), D_MODEL)
    gdn_conv_w = dense(ks[6], (N_A_LAYERS, GDN_CONV, GDN_CONV_DIM), GDN_CONV)
    gdn_a_log = jnp.log(jax.random.uniform(ks[7], (N_A_LAYERS, GDN_V_HEADS), f32, 1.0, 16.0))
    dt = jnp.exp(jax.random.uniform(ks[8], (N_A_LAYERS, GDN_V_HEADS), f32, math.log(1e-3), math.log(1e-1)))
    gdn_dt_bias = dt + jnp.log(-jnp.expm1(-dt))
    gdn_out_norm_g = gain(ks[9], (N_A_LAYERS, GDN_HEAD_DIM))
    gdn_w_out = dense(ks[10], (N_A_LAYERS, GDN_V_DIM, D_MODEL), GDN_V_DIM)
    kv_norm_g = gain(ks[11], (D_MODEL,))
    kv_w = dense(ks[12], (D_MODEL, NSA_KV_DIM), D_MODEL)
    cmp_pos = 0.02 * jax.random.normal(ks[13], (CMP_BLOCK, NSA_HEAD_DIM), f32)
    cmp_w1_k = dense(ks[14], (CMP_BLOCK * NSA_HEAD_DIM, CMP_HIDDEN), CMP_BLOCK * NSA_HEAD_DIM)
    cmp_w2_k = dense(ks[15], (CMP_HIDDEN, NSA_HEAD_DIM), CMP_HIDDEN)
    cmp_w1_v = dense(ks[16], (CMP_BLOCK * NSA_HEAD_DIM, CMP_HIDDEN), CMP_BLOCK * NSA_HEAD_DIM)
    cmp_w2_v = dense(ks[17], (CMP_HIDDEN, NSA_HEAD_DIM), CMP_HIDDEN)
    nsa_w_in = dense(ks[18], (N_B_LAYERS, D_MODEL, NSA_IN_DIM), D_MODEL)
    nsa_w_out = dense(ks[19], (N_B_LAYERS, NSA_Q_DIM, D_MODEL), NSA_Q_DIM)
    final_norm_g = gain(ks[20], (D_MODEL,))
    return {"x": x, "mixer_norm_g": mixer_norm_g, "ffn_norm_g": ffn_norm_g,
            "ffn_w_gate_up": ffn_w_gate_up, "ffn_w_down": ffn_w_down,
            "gdn_w_in": gdn_w_in, "gdn_conv_w": gdn_conv_w, "gdn_a_log": gdn_a_log,
            "gdn_dt_bias": gdn_dt_bias, "gdn_out_norm_g": gdn_out_norm_g, "gdn_w_out": gdn_w_out,
            "kv_norm_g": kv_norm_g, "kv_w": kv_w, "cmp_pos": cmp_pos,
            "cmp_w1_k": cmp_w1_k, "cmp_w2_k": cmp_w2_k, "cmp_w1_v": cmp_w1_v, "cmp_w2_v": cmp_w2_v,
            "nsa_w_in": nsa_w_in, "nsa_w_out": nsa_w_out, "final_norm_g": final_norm_g}


def reference(x, mixer_norm_g, ffn_norm_g, ffn_w_gate_up, ffn_w_down,
              gdn_w_in, gdn_conv_w, gdn_a_log, gdn_dt_bias, gdn_out_norm_g, gdn_w_out,
              kv_norm_g, kv_w, cmp_pos, cmp_w1_k, cmp_w2_k, cmp_w1_v, cmp_w2_v,
              nsa_w_in, nsa_w_out, final_norm_g):
    h = x
    shared_kv = None
    for layer in range(DEPTH):
        if layer < N_A_LAYERS:
            h = h + gated_deltanet(rms_norm(h, mixer_norm_g[layer]), gdn_w_in[layer], gdn_conv_w[layer],
                                   gdn_a_log[layer], gdn_dt_bias[layer], gdn_out_norm_g[layer], gdn_w_out[layer])
        else:
            j = layer - N_A_LAYERS
            h = h + nsa_attention(rms_norm(h, mixer_norm_g[layer]), shared_kv, nsa_w_in[j], nsa_w_out[j])
        h = h + swiglu_ffn(rms_norm(h, ffn_norm_g[layer]), ffn_w_gate_up[layer], ffn_w_down[layer])
        if layer == N_A_LAYERS - 1:
            shared_kv = nsa_shared_kv(h, kv_norm_g, kv_w, cmp_pos, cmp_w1_k, cmp_w2_k, cmp_w1_v, cmp_w2_v)
    return rms_norm(h, final_norm_g)
```

```python
import functools

import jax
import jax.numpy as jnp
from jax import lax
from jax.experimental import pallas as pl
from jax.experimental.pallas import tpu as pltpu

F32 = jnp.float32
BF16 = jnp.bfloat16

RMS_EPS = 1e-6
L2_EPS = 1e-6
NEG_INF = -1e30
FORCE_SCORE = 1e4

GDN_QK_HEADS = 16
GDN_V_HEADS = 32
HEAD_DIM = 128
GDN_CONV = 4
GDN_CHUNK = 64

NSA_HEADS = 16
NSA_GROUPS = 4
NSA_HPG = NSA_HEADS // NSA_GROUPS
CMP_BLOCK = 32
CMP_STRIDE = 16
SEL_BLOCK = 64
SEL_COUNT = 16
WINDOW = 512
Q_BLOCK = 128
SEL_RATIO = SEL_BLOCK // CMP_STRIDE
SEL_OVERLAP_W = (1.0, 2.0, 2.0, 2.0, 1.0)

V7X_VMEM_LIMIT_BYTES = 56 * 1024 * 1024
LANES = 128
SUBLANES = 8


def _params(sem):
    return pltpu.CompilerParams(dimension_semantics=sem, vmem_limit_bytes=V7X_VMEM_LIMIT_BYTES)


def _sigmoid(x):
    return 1.0 / (1.0 + jnp.exp(-x))


def _dot(a, b):
    return jnp.dot(a, b, preferred_element_type=F32)


def _dot_nt(a, b):
    return lax.dot_general(a, b, (((1,), (1,)), ((), ())), preferred_element_type=F32)


def _split_bf16(a):
    hi = a.astype(BF16)
    lo = (a - hi.astype(F32)).astype(BF16)
    return hi, lo


def _dot_hi(a, b):
    ah, al = _split_bf16(a)
    bh, bl = _split_bf16(b)
    return _dot(ah, bh) + (_dot(ah, bl) + _dot(al, bh))


def _norm_rows(x_ref, g_ref, xn_ref):
    x = x_ref[...]
    ms = jnp.mean(x * x, axis=-1, keepdims=True)
    xn_ref[...] = (x * lax.rsqrt(ms + RMS_EPS) * g_ref[...]).astype(xn_ref.dtype)


def _norm_mm_kernel(x_ref, g_ref, w_ref, o_ref, xn_ref, *, scale, split):
    @pl.when(pl.program_id(1) == 0)
    def _():
        _norm_rows(x_ref, g_ref, xn_ref)

    acc = _dot(xn_ref[...], w_ref[...])
    if scale is not None:
        acc = acc * scale
    if split:
        for c in range(o_ref.shape[0]):
            o_ref[c] = acc[:, c * LANES:(c + 1) * LANES].astype(o_ref.dtype)
    else:
        o_ref[...] = acc.astype(o_ref.dtype)


def norm_matmul(x, g, w, *, out_dtype, scale=None, split=False, tm=1024, tn=512):
    T, D = x.shape
    N = w.shape[1]
    tn = min(tn, N)
    assert T % tm == 0 and N % tn == 0 and tn % LANES == 0
    if split:
        out_shape = jax.ShapeDtypeStruct((N // LANES, T, LANES), out_dtype)
        out_spec = pl.BlockSpec((tn // LANES, tm, LANES), lambda i, j: (j, i, 0))
    else:
        out_shape = jax.ShapeDtypeStruct((T, N), out_dtype)
        out_spec = pl.BlockSpec((tm, tn), lambda i, j: (i, j))
    return pl.pallas_call(
        functools.partial(_norm_mm_kernel, scale=scale, split=split),
        grid=(T // tm, N // tn),
        in_specs=[pl.BlockSpec((tm, D), lambda i, j: (i, 0)),
                  pl.BlockSpec((1, D), lambda i, j: (0, 0)),
                  pl.BlockSpec((D, tn), lambda i, j: (0, j))],
        out_specs=out_spec,
        out_shape=out_shape,
        scratch_shapes=[pltpu.VMEM((tm, D), BF16)],
        compiler_params=_params(("arbitrary", "arbitrary")),
        name="norm_matmul",
    )(x, g.reshape(1, D), w)


def _norm_swiglu_kernel(x_ref, g_ref, wg_ref, wu_ref, o_ref, xn_ref):
    @pl.when(pl.program_id(1) == 0)
    def _():
        _norm_rows(x_ref, g_ref, xn_ref)

    xn = xn_ref[...]
    gate = _dot(xn, wg_ref[...])
    up = _dot(xn, wu_ref[...])
    o_ref[...] = (gate * _sigmoid(gate) * up).astype(o_ref.dtype)


def norm_swiglu(x, g, w_gate_up, *, tm=1024, tn=512):
    T, D = x.shape
    F = w_gate_up.shape[1] // 2
    assert T % tm == 0 and F % tn == 0
    nj = F // tn
    return pl.pallas_call(
        _norm_swiglu_kernel,
        grid=(T // tm, nj),
        in_specs=[pl.BlockSpec((tm, D), lambda i, j: (i, 0)),
                  pl.BlockSpec((1, D), lambda i, j: (0, 0)),
                  pl.BlockSpec((D, tn), lambda i, j: (0, j)),
                  pl.BlockSpec((D, tn), lambda i, j: (0, j + nj))],
        out_specs=pl.BlockSpec((tm, tn), lambda i, j: (i, j)),
        out_shape=jax.ShapeDtypeStruct((T, F), BF16),
        scratch_shapes=[pltpu.VMEM((tm, D), BF16)],
        compiler_params=_params(("arbitrary", "arbitrary")),
        name="norm_swiglu",
    )(x, g.reshape(1, D), w_gate_up, w_gate_up)


def _mm_res_kernel(a_ref, w_ref, r_ref, o_ref):
    o_ref[...] = r_ref[...] + _dot(a_ref[...], w_ref[...])


def matmul_residual(a, w, res, *, tm=512, tn=512):
    T, K = a.shape
    N = w.shape[1]
    assert T % tm == 0 and N % tn == 0
    return pl.pallas_call(
        _mm_res_kernel,
        grid=(T // tm, N // tn),
        in_specs=[pl.BlockSpec((tm, K), lambda i, j: (i, 0)),
                  pl.BlockSpec((K, tn), lambda i, j: (0, j)),
                  pl.BlockSpec((tm, tn), lambda i, j: (i, j))],
        out_specs=pl.BlockSpec((tm, tn), lambda i, j: (i, j)),
        out_shape=jax.ShapeDtypeStruct((T, N), F32),
        compiler_params=_params(("arbitrary", "arbitrary")),
        name="matmul_residual",
    )(a, w, res)


def _rmsnorm_kernel(x_ref, g_ref, o_ref):
    _norm_rows(x_ref, g_ref, o_ref)


def rmsnorm(x, g, *, tm=512):
    T, D = x.shape
    return pl.pallas_call(
        _rmsnorm_kernel,
        grid=(T // tm,),
        in_specs=[pl.BlockSpec((tm, D), lambda i: (i, 0)),
                  pl.BlockSpec((1, D), lambda i: (0, 0))],
        out_specs=pl.BlockSpec((tm, D), lambda i: (i, 0)),
        out_shape=jax.ShapeDtypeStruct((T, D), x.dtype),
        compiler_params=_params(("arbitrary",)),
        name="final_rmsnorm",
    )(x, g.reshape(1, D))


def _conv_kernel(x_ref, halo_ref, w_ref, o_ref, *, n_qk_blocks, heads_per_block):
    i = pl.program_id(0)
    j = pl.program_id(1)
    x = x_ref[...]
    halo = jnp.where(i == 0, 0.0, halo_ref[...])
    w = w_ref[...]
    top = x[:SUBLANES]
    row8 = lax.broadcasted_iota(jnp.int32, top.shape, 0)
    y = w[GDN_CONV - 1:GDN_CONV] * x
    y_top = w[GDN_CONV - 1:GDN_CONV] * top
    for s in range(1, GDN_CONV):
        wk = w[GDN_CONV - 1 - s:GDN_CONV - s]
        xs = pltpu.roll(x, s, 0)
        y = y + wk * xs
        hs = pltpu.roll(halo, s, 0)
        y_top = y_top + wk * jnp.where(row8 < s, hs, xs[:SUBLANES])
    is_qk = j < n_qk_blocks

    def finish(v):
        v = v * _sigmoid(v)
        outs = []
        for hh in range(heads_per_block):
            vh = v[:, hh * HEAD_DIM:(hh + 1) * HEAD_DIM]
            inv = lax.rsqrt(jnp.sum(vh * vh, axis=-1, keepdims=True) + L2_EPS)
            outs.append(vh * jnp.where(is_qk, inv, 1.0))
        return jnp.concatenate(outs, axis=-1) if len(outs) > 1 else outs[0]

    o_ref[...] = finish(y).astype(o_ref.dtype)
    o_ref[:SUBLANES, :] = finish(y_top).astype(o_ref.dtype)


def gdn_conv(proj, conv_w, *, n_qk_cols, tb=512, cb=512):
    T = proj.shape[0]
    C = conv_w.shape[1]
    assert T % tb == 0 and C % cb == 0 and n_qk_cols % cb == 0 and tb % SUBLANES == 0
    hb = tb // SUBLANES
    return pl.pallas_call(
        functools.partial(_conv_kernel, n_qk_blocks=n_qk_cols // cb, heads_per_block=cb // HEAD_DIM),
        grid=(T // tb, C // cb),
        in_specs=[pl.BlockSpec((tb, cb), lambda i, j: (i, j)),
                  pl.BlockSpec((SUBLANES, cb), lambda i, j: (jnp.maximum(i * hb - 1, 0), j)),
                  pl.BlockSpec((GDN_CONV, cb), lambda i, j: (0, j))],
        out_specs=pl.BlockSpec((tb, cb), lambda i, j: (i, j)),
        out_shape=jax.ShapeDtypeStruct((T, C), F32),
        compiler_params=_params(("arbitrary", "arbitrary")),
        name="gdn_conv",
    )(proj, proj, conv_w)


def _gate_kernel(ba_ref, alog_ref, dtb_ref, beta_ref, gc_ref):
    nh = beta_ref.shape[0]
    b_raw = ba_ref[:nh, :]
    a_raw = ba_ref[nh:2 * nh, :]
    beta_ref[...] = _sigmoid(b_raw)
    xs = a_raw + dtb_ref[...]
    softplus = jnp.maximum(xs, 0.0) + jnp.log(1.0 + jnp.exp(-jnp.abs(xs)))
    g = -jnp.exp(alog_ref[...]) * softplus
    pos = lax.broadcasted_iota(jnp.int32, g.shape, 1) % GDN_CHUNK
    s = 1
    while s < GDN_CHUNK:
        g = g + jnp.where(pos >= s, pltpu.roll(g, s, 1), 0.0)
        s *= 2
    gc_ref[...] = g


def gdn_gates(ba_t, a_log, dt_bias, *, tl=2048):
    nh2, T = ba_t.shape
    nh = nh2 // 2
    tl = min(tl, T)
    assert T % tl == 0 and tl % GDN_CHUNK == 0
    return pl.pallas_call(
        _gate_kernel,
        grid=(T // tl,),
        in_specs=[pl.BlockSpec((nh2, tl), lambda i: (0, i)),
                  pl.BlockSpec((nh, 1), lambda i: (0, 0)),
                  pl.BlockSpec((nh, 1), lambda i: (0, 0))],
        out_specs=[pl.BlockSpec((nh, tl), lambda i: (0, i)),
                   pl.BlockSpec((nh, tl), lambda i: (0, i))],
        out_shape=[jax.ShapeDtypeStruct((nh, T), F32), jax.ShapeDtypeStruct((nh, T), F32)],
        compiler_params=_params(("arbitrary",)),
        name="gdn_gates",
    )(ba_t, a_log.reshape(nh, 1), dt_bias.reshape(nh, 1))


def _gdn_kernel(q_ref, k_ref, v_ref, z_ref, gcol_ref, bcol_ref, grow_ref, ong_ref, o_ref,
                s_ref, u_ref, w_ref, attn_ref, qd_ref, kdt_ref, egl_ref):
    h = pl.program_id(0)
    c = pl.program_id(1)
    C = GDN_CHUNK
    nc = u_ref.shape[0]

    @pl.when(c == 0)
    def _():
        s_ref[...] = jnp.zeros_like(s_ref)

    lane = lax.broadcasted_iota(jnp.int32, gcol_ref.shape, 1)
    gcol = jnp.sum(jnp.where(lane == h, gcol_ref[...], 0.0), axis=1, keepdims=True)
    bcol = jnp.sum(jnp.where(lane == h, bcol_ref[...], 0.0), axis=1, keepdims=True)
    ri = lax.broadcasted_iota(jnp.int32, (C, C), 0)
    ci = lax.broadcasted_iota(jnp.int32, (C, C), 1)
    causal = ri >= ci
    strict = ri > ci
    eye = jnp.where(ri == ci, 1.0, 0.0)
    qscale = HEAD_DIM ** -0.5

    for n in range(nc):
        sl = slice(n * C, (n + 1) * C)
        qn = q_ref[sl, :] * qscale
        kn = k_ref[sl, :]
        vn = v_ref[sl, :]
        gc = gcol[sl]
        bt = bcol[sl]
        gr = grow_ref[0, n:n + 1, :]
        decay = jnp.where(causal, jnp.exp(jnp.where(causal, gc - gr, 0.0)), 0.0)
        kb = kn * bt
        kn16 = kn.astype(BF16)
        lmat = jnp.where(strict, _dot_nt(kb.astype(BF16), kn16) * decay, 0.0)
        inv = eye - lmat
        m = lmat
        for _ in range(5):
            m = _dot_hi(m, m)
            inv = inv + _dot_hi(inv, m)
        eg = jnp.exp(gc)
        u_ref[n] = _dot_hi(inv, vn * bt)
        w_ref[n] = _dot_hi(inv, kb * eg).astype(w_ref.dtype)
        attn_ref[n] = jnp.where(causal, _dot_nt(qn.astype(BF16), kn16) * decay, 0.0).astype(attn_ref.dtype)
        qd_ref[n] = (qn * eg).astype(qd_ref.dtype)
        gl = gc[C - 1:C]
        kdt_ref[n] = (kn * jnp.exp(gl - gc)).T.astype(kdt_ref.dtype)
        egl_ref[n] = jnp.broadcast_to(jnp.exp(gl), egl_ref.shape[1:])

    state = s_ref[...]
    ong = ong_ref[...]
    for n in range(nc):
        sl = slice(n * C, (n + 1) * C)
        s16 = state.astype(BF16)
        v_new = u_ref[n] - _dot(w_ref[n], s16)
        v16 = v_new.astype(BF16)
        o = _dot(qd_ref[n], s16) + _dot(attn_ref[n], v16)
        state = state * egl_ref[n][:1, :] + _dot(kdt_ref[n], v16)
        ms = jnp.mean(o * o, axis=-1, keepdims=True)
        on = o * lax.rsqrt(ms + RMS_EPS) * ong
        z = z_ref[sl, :]
        o_ref[sl, :] = (on * (z * _sigmoid(z))).astype(o_ref.dtype)
    s_ref[...] = state


def gdn_delta_rule(qkv, proj, gc_col, beta_col, gc_row, out_norm_g, *, tb=512):
    T = qkv.shape[0]
    H = GDN_V_HEADS
    rep = GDN_V_HEADS // GDN_QK_HEADS
    kq = GDN_QK_HEADS
    C = GDN_CHUNK
    assert T % tb == 0 and tb % C == 0
    nc = tb // C
    zoff = (2 * GDN_QK_HEADS + GDN_V_HEADS)
    return pl.pallas_call(
        _gdn_kernel,
        grid=(H, T // tb),
        in_specs=[pl.BlockSpec((tb, HEAD_DIM), lambda h, c: (c, h // rep)),
                  pl.BlockSpec((tb, HEAD_DIM), lambda h, c: (c, kq + h // rep)),
                  pl.BlockSpec((tb, HEAD_DIM), lambda h, c: (c, 2 * kq + h)),
                  pl.BlockSpec((tb, HEAD_DIM), lambda h, c: (c, zoff + h)),
                  pl.BlockSpec((tb, H), lambda h, c: (c, 0)),
                  pl.BlockSpec((tb, H), lambda h, c: (c, 0)),
                  pl.BlockSpec((1, nc, C), lambda h, c: (h, c, 0)),
                  pl.BlockSpec((1, HEAD_DIM), lambda h, c: (0, 0))],
        out_specs=pl.BlockSpec((tb, HEAD_DIM), lambda h, c: (c, h)),
        out_shape=jax.ShapeDtypeStruct((T, H * HEAD_DIM), BF16),
        scratch_shapes=[pltpu.VMEM((HEAD_DIM, HEAD_DIM), F32),
                        pltpu.VMEM((nc, C, HEAD_DIM), F32),
                        pltpu.VMEM((nc, C, HEAD_DIM), BF16),
                        pltpu.VMEM((nc, C, C), BF16),
                        pltpu.VMEM((nc, C, HEAD_DIM), BF16),
                        pltpu.VMEM((nc, HEAD_DIM, C), BF16),
                        pltpu.VMEM((nc, SUBLANES, HEAD_DIM), F32)],
        compiler_params=_params(("arbitrary", "arbitrary")),
        name="gdn_delta_rule",
    )(qkv, qkv, qkv, proj, gc_col, beta_col, gc_row.reshape(H, T // C, C), out_norm_g.reshape(1, HEAD_DIM))


def _cmp_kernel(x_ref, pos_ref, w1_ref, w2_ref, o_ref):
    x = x_ref[...]
    half = x.shape[1]
    top = (x + pos_ref[0:1, :]).astype(BF16)
    bot = (x + pos_ref[1:2, :]).astype(BF16)
    a = _dot(top, w1_ref[:half, :])
    b = _dot(bot, w1_ref[half:, :])
    pre = a + pltpu.roll(b, x.shape[0] - 1, 0)
    hid = pre * _sigmoid(pre)
    o_ref[...] = _dot(hid.astype(BF16), w2_ref[...]).astype(o_ref.dtype)


def compress_kv(kvc, pos, w1, w2):
    n8, nseg, half = kvc.shape
    G = n8 // 2
    hid = w1.shape[2]
    return pl.pallas_call(
        _cmp_kernel,
        grid=(n8,),
        in_specs=[pl.BlockSpec((None, nseg, half), lambda i: (i, 0, 0)),
                  pl.BlockSpec((2, half), lambda i: (0, 0)),
                  pl.BlockSpec((None, 2 * half, hid), lambda i: (i // G, 0, 0)),
                  pl.BlockSpec((None, hid, HEAD_DIM), lambda i: (i // G, 0, 0))],
        out_specs=pl.BlockSpec((None, nseg, HEAD_DIM), lambda i: (i, 0, 0)),
        out_shape=jax.ShapeDtypeStruct((n8, nseg, HEAD_DIM), BF16),
        compiler_params=_params(("arbitrary",)),
        name="nsa_compress",
    )(kvc, pos, w1, w2)


def _softmax_rows(s, valid):
    m = jnp.max(jnp.where(valid, s, NEG_INF), axis=-1, keepdims=True)
    e = jnp.where(valid, jnp.exp(jnp.where(valid, s - m, 0.0)), 0.0)
    den = jnp.sum(e, axis=-1, keepdims=True)
    return e / jnp.where(den > 0.0, den, 1.0)


def _nsa_kernel(q_ref, gate_ref, slope_ref, kc_ref, vc_ref, ks_ref, vs_ref, kw_ref, vw_ref,
                wsel_ref, exp_ref, o_ref, *, n_sel, tk):
    qb = pl.program_id(1)
    start = qb * Q_BLOCK
    HG = NSA_HPG
    R = HG * Q_BLOCK
    D = HEAD_DIM
    q = jnp.concatenate([q_ref[:, hh * D:(hh + 1) * D] for hh in range(HG)], axis=0)
    slope = slope_ref[...]
    tq = start + lax.broadcasted_iota(jnp.int32, (Q_BLOCK, 1), 0)
    tq_f = tq.astype(F32)

    ncb = kc_ref.shape[0]
    cidx = lax.broadcasted_iota(jnp.int32, (1, ncb), 1)
    centre = cidx.astype(F32) * CMP_STRIDE + (CMP_BLOCK - 1) / 2
    valid_c = ((cidx * CMP_STRIDE + (CMP_BLOCK - 1)) <= tq)[None]
    s_c = _dot_nt(q, kc_ref[...]).reshape(HG, Q_BLOCK, ncb) - slope * (tq_f - centre)[None]
    p_c = _softmax_rows(s_c, valid_c)
    o_cmp = _dot(p_c.reshape(R, ncb).astype(BF16), vc_ref[...])
    imp = jnp.sum(p_c, axis=0)

    nsb = wsel_ref.shape[1]
    wsel = wsel_ref[...]
    i1 = imp.astype(BF16)
    r1 = imp - i1.astype(F32)
    i2 = r1.astype(BF16)
    i3 = (r1 - i2.astype(F32)).astype(BF16)
    imp_slc = _dot(i1, wsel) + (_dot(i2, wsel) + _dot(i3, wsel))
    blk = lax.broadcasted_iota(jnp.int32, (Q_BLOCK, nsb), 1)
    cur = tq // SEL_BLOCK
    forced = (blk == 0) | (blk == cur) | (blk == cur - 1)
    visible = blk * SEL_BLOCK <= tq
    score = jnp.where(visible, jnp.where(forced, FORCE_SCORE, imp_slc), NEG_INF)
    selected = jnp.zeros((Q_BLOCK, nsb), F32)
    for _ in range(n_sel):
        mx = jnp.max(score, axis=-1, keepdims=True)
        first = jnp.min(jnp.where(score == mx, blk, nsb), axis=-1, keepdims=True)
        pick = blk == first
        selected = jnp.where(pick, 1.0, selected)
        score = jnp.where(pick, -jnp.inf, score)

    bpt = tk // SEL_BLOCK
    n_tiles = (start + Q_BLOCK - 1) // tk + 1
    expand = exp_ref[...]

    def sel_tile(kt, carry):
        m_run, l_run, acc = carry
        k0 = pl.multiple_of(kt * tk, tk)
        shifted = pltpu.roll(selected, nsb - kt * bpt, 1)
        sel_keys = _dot(shifted[:, :LANES].astype(BF16), expand)
        kpos = k0 + lax.broadcasted_iota(jnp.int32, (1, tk), 1)
        ok = ((sel_keys > 0.5) & (kpos <= tq))[None]
        dist = (tq_f - kpos.astype(F32))[None]
        s = _dot_nt(q, ks_ref[pl.ds(k0, tk), :]).reshape(HG, Q_BLOCK, tk) - slope * dist
        s = jnp.where(ok, s, NEG_INF)
        m_new = jnp.maximum(m_run, jnp.max(s, axis=-1, keepdims=True))
        p = jnp.where(ok, jnp.exp(s - m_new), 0.0)
        alpha = jnp.exp(m_run - m_new)
        l_new = alpha * l_run + jnp.sum(p, axis=-1, keepdims=True)
        pv = _dot(p.reshape(R, tk).astype(BF16), vs_ref[pl.ds(k0, tk), :])
        acc = alpha * acc + pv.reshape(HG, Q_BLOCK, D)
        return m_new, l_new, acc

    init = (jnp.full((HG, Q_BLOCK, 1), NEG_INF, F32), jnp.zeros((HG, Q_BLOCK, 1), F32),
            jnp.zeros((HG, Q_BLOCK, D), F32))
    _, l_s, acc_s = lax.fori_loop(0, n_tiles, sel_tile, init)
    o_slc = acc_s / l_s

    wk = WINDOW + Q_BLOCK
    w0 = pl.multiple_of(jnp.maximum(start - WINDOW, 0), Q_BLOCK)
    kpos_w = w0 + lax.broadcasted_iota(jnp.int32, (1, wk), 1)
    dist_w = tq - kpos_w
    valid_w = ((dist_w >= 0) & (dist_w < WINDOW))[None]
    s_w = _dot_nt(q, kw_ref[pl.ds(w0, wk), :]).reshape(HG, Q_BLOCK, wk) - slope * dist_w.astype(F32)[None]
    p_w = _softmax_rows(s_w, valid_w)
    o_win = _dot(p_w.reshape(R, wk).astype(BF16), vw_ref[pl.ds(w0, wk), :])

    gates = _sigmoid(gate_ref[...])
    for hh in range(HG):
        rows = slice(hh * Q_BLOCK, (hh + 1) * Q_BLOCK)
        out = (gates[:, hh:hh + 1] * o_cmp[rows]
               + gates[:, HG + hh:HG + hh + 1] * o_slc[hh]
               + gates[:, 2 * HG + hh:2 * HG + hh + 1] * o_win[rows])
        o_ref[:, hh * D:(hh + 1) * D] = out.astype(o_ref.dtype)


def nsa_attention(q, gate_logits, slopes, k_cmp, v_cmp, k_slc, v_slc, k_win, v_win, *, tk=512):
    T = q.shape[0]
    G, HG, D = NSA_GROUPS, NSA_HPG, HEAD_DIM
    nqb = T // Q_BLOCK
    ncb = k_cmp.shape[1]
    nsb = T // SEL_BLOCK
    n_sel = min(SEL_COUNT, nsb)
    assert T % tk == 0 and tk % SEL_BLOCK == 0 and T >= WINDOW + Q_BLOCK and tk // SEL_BLOCK <= LANES
    nsb_pad = -(-nsb // LANES) * LANES
    cc = jnp.arange(ncb)[:, None]
    jj = jnp.arange(nsb_pad)[None, :]
    off = cc - (SEL_RATIO * jj - 1)
    wsel = jnp.zeros((ncb, nsb_pad), F32)
    for o, wt in enumerate(SEL_OVERLAP_W):
        wsel = jnp.where((off == o) & (jj < nsb), wt, wsel)
    wsel = wsel.astype(BF16)
    expand = (jnp.arange(LANES)[:, None] == (jnp.arange(tk)[None, :] // SEL_BLOCK)).astype(BF16)
    full = lambda rows: pl.BlockSpec((None, rows, D), lambda g, b: (g, 0, 0))
    return pl.pallas_call(
        functools.partial(_nsa_kernel, n_sel=n_sel, tk=tk),
        grid=(G, nqb),
        in_specs=[pl.BlockSpec((Q_BLOCK, HG * D), lambda g, b: (b, g)),
                  pl.BlockSpec((None, Q_BLOCK, 3 * HG), lambda g, b: (g, b, 0)),
                  pl.BlockSpec((None, HG, 1, 1), lambda g, b: (g, 0, 0, 0)),
                  full(ncb), full(ncb), full(T), full(T), full(T), full(T),
                  pl.BlockSpec((ncb, nsb_pad), lambda g, b: (0, 0)),
                  pl.BlockSpec((LANES, tk), lambda g, b: (0, 0))],
        out_specs=pl.BlockSpec((Q_BLOCK, HG * D), lambda g, b: (b, g)),
        out_shape=jax.ShapeDtypeStruct((T, NSA_HEADS * D), BF16),
        compiler_params=_params(("arbitrary", "arbitrary")),
        name="nsa_attention",
    )(q, gate_logits, slopes.reshape(G, HG, 1, 1), k_cmp, v_cmp, k_slc, v_slc, k_win, v_win, wsel, expand)


def _pad_cols(w, n):
    return jnp.pad(w, ((0, 0), (0, n - w.shape[1])))


def kernel(x, mixer_norm_g, ffn_norm_g, ffn_w_gate_up, ffn_w_down, gdn_w_in, gdn_conv_w, gdn_a_log, gdn_dt_bias, gdn_out_norm_g, gdn_w_out, kv_norm_g, kv_w, cmp_pos, cmp_w1_k, cmp_w2_k, cmp_w1_v, cmp_w2_v, nsa_w_in, nsa_w_out, final_norm_g):
    B, T, Dm = x.shape
    assert B == 1
    h = x.reshape(T, Dm)
    D = HEAD_DIM
    qk_dim = GDN_QK_HEADS * D
    v_dim = GDN_V_HEADS * D
    conv_dim = 2 * qk_dim + v_dim
    main = conv_dim + v_dim

    w_in = gdn_w_in[0]
    proj = norm_matmul(h, mixer_norm_g[0], w_in[:, :main].astype(BF16), out_dtype=F32)
    ba = norm_matmul(h, mixer_norm_g[0], _pad_cols(w_in[:, main:], LANES).astype(BF16), out_dtype=F32)
    beta_t, gc_t = gdn_gates(ba[:, :2 * GDN_V_HEADS].T, gdn_a_log[0], gdn_dt_bias[0])
    qkv = gdn_conv(proj, gdn_conv_w[0], n_qk_cols=2 * qk_dim)
    o = gdn_delta_rule(qkv, proj, gc_t.T, beta_t.T, gc_t, gdn_out_norm_g[0])
    h = matmul_residual(o, gdn_w_out[0].astype(BF16), h)
    act = norm_swiglu(h, ffn_norm_g[0], ffn_w_gate_up[0].astype(BF16))
    h = matmul_residual(act, ffn_w_down[0].astype(BF16), h)

    G = NSA_GROUPS
    n_cmp_cols = 2 * G * D
    kv_w16 = kv_w.astype(BF16)
    kvc = norm_matmul(h, kv_norm_g, kv_w16[:, :n_cmp_cols], out_dtype=F32, split=True)
    kvr = norm_matmul(h, kv_norm_g, kv_w16[:, n_cmp_cols:], out_dtype=BF16, split=True)
    nseg = T // CMP_STRIDE
    pos2 = cmp_pos.reshape(2, CMP_STRIDE * D)
    w1 = jnp.stack([cmp_w1_k, cmp_w1_v]).astype(BF16)
    w2 = jnp.stack([cmp_w2_k, cmp_w2_v]).astype(BF16)
    kv_cmp = compress_kv(kvc.reshape(2 * G, nseg, CMP_STRIDE * D), pos2, w1, w2)

    w_nsa = nsa_w_in[0]
    q_dim = NSA_HEADS * D
    q = norm_matmul(h, mixer_norm_g[1], w_nsa[:, :q_dim].astype(BF16), out_dtype=BF16, scale=D ** -0.5)
    gl = norm_matmul(h, mixer_norm_g[1], _pad_cols(w_nsa[:, q_dim:], LANES).astype(BF16), out_dtype=F32)
    gl = gl[:, :3 * NSA_HEADS].reshape(T, 3, G, NSA_HPG).transpose(2, 0, 1, 3).reshape(G, T, 3 * NSA_HPG)
    slopes = 2.0 ** (-8.0 * jnp.arange(1, NSA_HEADS + 1, dtype=F32) / NSA_HEADS)
    o = nsa_attention(q, gl, slopes, kv_cmp[:G], kv_cmp[G:], kvr[:G], kvr[G:2 * G], kvr[2 * G:3 * G], kvr[3 * G:])
    h = matmul_residual(o, nsa_w_out[0].astype(BF16), h)
    act = norm_swiglu(h, ffn_norm_g[1], ffn_w_gate_up[1].astype(BF16))
    h = matmul_residual(act, ffn_w_down[1].astype(BF16), h)
    return rmsnorm(h, final_norm_g).reshape(B, T, Dm)
```

```python
import functools

import jax
import jax.numpy as jnp
from jax import lax
from jax.experimental import pallas as pl
from jax.experimental.pallas import tpu as pltpu

F32 = jnp.float32
BF16 = jnp.bfloat16

RMS_EPS = 1e-6
L2_EPS = 1e-6
NEG_INF = -1e30
FORCE_SCORE = 1e4

GDN_QK_HEADS = 16
GDN_V_HEADS = 32
HEAD_DIM = 128
GDN_CONV = 4
GDN_CHUNK = 64

NSA_HEADS = 16
NSA_GROUPS = 4
NSA_HPG = NSA_HEADS // NSA_GROUPS
CMP_BLOCK = 32
CMP_STRIDE = 16
SEL_BLOCK = 64
SEL_COUNT = 16
WINDOW = 512
Q_BLOCK = 128
SEL_RATIO = SEL_BLOCK // CMP_STRIDE
SEL_OVERLAP_W = (1.0, 2.0, 2.0, 2.0, 1.0)

V7X_VMEM_LIMIT_BYTES = 56 * 1024 * 1024
LANES = 128
SUBLANES = 8


def _params(sem):
    return pltpu.CompilerParams(dimension_semantics=sem, vmem_limit_bytes=V7X_VMEM_LIMIT_BYTES)


def _sigmoid(x):
    return 1.0 / (1.0 + jnp.exp(-x))


def _dot(a, b):
    return jnp.dot(a, b, preferred_element_type=F32)


def _dot_nt(a, b):
    return lax.dot_general(a, b, (((1,), (1,)), ((), ())), preferred_element_type=F32)


def _split_bf16(a):
    hi = a.astype(BF16)
    lo = (a - hi.astype(F32)).astype(BF16)
    return hi, lo


def _dot_hi(a, b):
    ah, al = _split_bf16(a)
    bh, bl = _split_bf16(b)
    return _dot(ah, bh) + (_dot(ah, bl) + _dot(al, bh))


def _norm_rows(x_ref, g_ref, xn_ref):
    x = x_ref[...]
    ms = jnp.mean(x * x, axis=-1, keepdims=True)
    xn_ref[...] = (x * lax.rsqrt(ms + RMS_EPS) * g_ref[...]).astype(xn_ref.dtype)


def _norm_mm_kernel(x_ref, g_ref, w_ref, o_ref, xn_ref, *, scale, split):
    @pl.when(pl.program_id(1) == 0)
    def _():
        _norm_rows(x_ref, g_ref, xn_ref)

    acc = _dot(xn_ref[...], w_ref[...])
    if scale is not None:
        acc = acc * scale
    if split:
        for c in range(o_ref.shape[0]):
            o_ref[c] = acc[:, c * LANES:(c + 1) * LANES].astype(o_ref.dtype)
    else:
        o_ref[...] = acc.astype(o_ref.dtype)


def norm_matmul(x, g, w, *, out_dtype, scale=None, split=False, tm=1024, tn=512):
    T, D = x.shape
    N = w.shape[1]
    tn = min(tn, N)
    assert T % tm == 0 and N % tn == 0 and tn % LANES == 0
    if split:
        out_shape = jax.ShapeDtypeStruct((N // LANES, T, LANES), out_dtype)
        out_spec = pl.BlockSpec((tn // LANES, tm, LANES), lambda i, j: (j, i, 0))
    else:
        out_shape = jax.ShapeDtypeStruct((T, N), out_dtype)
        out_spec = pl.BlockSpec((tm, tn), lambda i, j: (i, j))
    return pl.pallas_call(
        functools.partial(_norm_mm_kernel, scale=scale, split=split),
        grid=(T // tm, N // tn),
        in_specs=[pl.BlockSpec((tm, D), lambda i, j: (i, 0)),
                  pl.BlockSpec((1, D), lambda i, j: (0, 0)),
                  pl.BlockSpec((D, tn), lambda i, j: (0, j))],
        out_specs=out_spec,
        out_shape=out_shape,
        scratch_shapes=[pltpu.VMEM((tm, D), BF16)],
        compiler_params=_params(("arbitrary", "arbitrary")),
        name="norm_matmul",
    )(x, g.reshape(1, D), w)


def _norm_swiglu_kernel(x_ref, g_ref, wg_ref, wu_ref, o_ref, xn_ref):
    @pl.when(pl.program_id(1) == 0)
    def _():
        _norm_rows(x_ref, g_ref, xn_ref)

    xn = xn_ref[...]
    gate = _dot(xn, wg_ref[...])
    up = _dot(xn, wu_ref[...])
    o_ref[...] = (gate * _sigmoid(gate) * up).astype(o_ref.dtype)


def norm_swiglu(x, g, w_gate_up, *, tm=1024, tn=512):
    T, D = x.shape
    F = w_gate_up.shape[1] // 2
    assert T % tm == 0 and F % tn == 0
    nj = F // tn
    return pl.pallas_call(
        _norm_swiglu_kernel,
        grid=(T // tm, nj),
        in_specs=[pl.BlockSpec((tm, D), lambda i, j: (i, 0)),
                  pl.BlockSpec((1, D), lambda i, j: (0, 0)),
                  pl.BlockSpec((D, tn), lambda i, j: (0, j)),
                  pl.BlockSpec((D, tn), lambda i, j: (0, j + nj))],
        out_specs=pl.BlockSpec((tm, tn), lambda i, j: (i, j)),
        out_shape=jax.ShapeDtypeStruct((T, F), BF16),
        scratch_shapes=[pltpu.VMEM((tm, D), BF16)],
        compiler_params=_params(("arbitrary", "arbitrary")),
        name="norm_swiglu",
    )(x, g.reshape(1, D), w_gate_up, w_gate_up)


def _mm_res_kernel(a_ref, w_ref, r_ref, o_ref):
    o_ref[...] = r_ref[...] + _dot(a_ref[...], w_ref[...])


def matmul_residual(a, w, res, *, tm=512, tn=512):
    T, K = a.shape
    N = w.shape[1]
    assert T % tm == 0 and N % tn == 0
    return pl.pallas_call(
        _mm_res_kernel,
        grid=(T // tm, N // tn),
        in_specs=[pl.BlockSpec((tm, K), lambda i, j: (i, 0)),
                  pl.BlockSpec((K, tn), lambda i, j: (0, j)),
                  pl.BlockSpec((tm, tn), lambda i, j: (i, j))],
        out_specs=pl.BlockSpec((tm, tn), lambda i, j: (i, j)),
        out_shape=jax.ShapeDtypeStruct((T, N), F32),
        compiler_params=_params(("arbitrary", "arbitrary")),
        name="matmul_residual",
    )(a, w, res)


def _rmsnorm_kernel(x_ref, g_ref, o_ref):
    _norm_rows(x_ref, g_ref, o_ref)


def rmsnorm(x, g, *, tm=512):
    T, D = x.shape
    return pl.pallas_call(
        _rmsnorm_kernel,
        grid=(T // tm,),
        in_specs=[pl.BlockSpec((tm, D), lambda i: (i, 0)),
                  pl.BlockSpec((1, D), lambda i: (0, 0))],
        out_specs=pl.BlockSpec((tm, D), lambda i: (i, 0)),
        out_shape=jax.ShapeDtypeStruct((T, D), x.dtype),
        compiler_params=_params(("arbitrary",)),
        name="final_rmsnorm",
    )(x, g.reshape(1, D))


def _conv_kernel(x_ref, halo_ref, w_ref, o_ref, *, n_qk_blocks, heads_per_block):
    i = pl.program_id(0)
    j = pl.program_id(1)
    x = x_ref[...]
    halo = jnp.where(i == 0, 0.0, halo_ref[...])
    w = w_ref[...]
    top = x[:SUBLANES]
    row8 = lax.broadcasted_iota(jnp.int32, top.shape, 0)
    y = w[GDN_CONV - 1:GDN_CONV] * x
    y_top = w[GDN_CONV - 1:GDN_CONV] * top
    for s in range(1, GDN_CONV):
        wk = w[GDN_CONV - 1 - s:GDN_CONV - s]
        xs = pltpu.roll(x, s, 0)
        y = y + wk * xs
        hs = pltpu.roll(halo, s, 0)
        y_top = y_top + wk * jnp.where(row8 < s, hs, xs[:SUBLANES])
    is_qk = j < n_qk_blocks

    def finish(v):
        v = v * _sigmoid(v)
        outs = []
        for hh in range(heads_per_block):
            vh = v[:, hh * HEAD_DIM:(hh + 1) * HEAD_DIM]
            inv = lax.rsqrt(jnp.sum(vh * vh, axis=-1, keepdims=True) + L2_EPS)
            outs.append(vh * jnp.where(is_qk, inv, 1.0))
        return jnp.concatenate(outs, axis=-1) if len(outs) > 1 else outs[0]

    o_ref[...] = finish(y).astype(o_ref.dtype)
    o_ref[:SUBLANES, :] = finish(y_top).astype(o_ref.dtype)


def gdn_conv(proj, conv_w, *, n_qk_cols, tb=512, cb=512):
    T = proj.shape[0]
    C = conv_w.shape[1]
    assert T % tb == 0 and C % cb == 0 and n_qk_cols % cb == 0 and tb % SUBLANES == 0
    hb = tb // SUBLANES
    return pl.pallas_call(
        functools.partial(_conv_kernel, n_qk_blocks=n_qk_cols // cb, heads_per_block=cb // HEAD_DIM),
        grid=(T // tb, C // cb),
        in_specs=[pl.BlockSpec((tb, cb), lambda i, j: (i, j)),
                  pl.BlockSpec((SUBLANES, cb), lambda i, j: (jnp.maximum(i * hb - 1, 0), j)),
                  pl.BlockSpec((GDN_CONV, cb), lambda i, j: (0, j))],
        out_specs=pl.BlockSpec((tb, cb), lambda i, j: (i, j)),
        out_shape=jax.ShapeDtypeStruct((T, C), F32),
        compiler_params=_params(("arbitrary", "arbitrary")),
        name="gdn_conv",
    )(proj, proj, conv_w)


def _gate_kernel(ba_ref, alog_ref, dtb_ref, beta_ref, gc_ref):
    nh = beta_ref.shape[0]
    b_raw = ba_ref[:nh, :]
    a_raw = ba_ref[nh:2 * nh, :]
    beta_ref[...] = _sigmoid(b_raw)
    xs = a_raw + dtb_ref[...]
    softplus = jnp.maximum(xs, 0.0) + jnp.log(1.0 + jnp.exp(-jnp.abs(xs)))
    g = -jnp.exp(alog_ref[...]) * softplus
    pos = lax.broadcasted_iota(jnp.int32, g.shape, 1) % GDN_CHUNK
    s = 1
    while s < GDN_CHUNK:
        g = g + jnp.where(pos >= s, pltpu.roll(g, s, 1), 0.0)
        s *= 2
    gc_ref[...] = g


def gdn_gates(ba_t, a_log, dt_bias, *, tl=2048):
    nh2, T = ba_t.shape
    nh = nh2 // 2
    tl = min(tl, T)
    assert T % tl == 0 and tl % GDN_CHUNK == 0
    return pl.pallas_call(
        _gate_kernel,
        grid=(T // tl,),
        in_specs=[pl.BlockSpec((nh2, tl), lambda i: (0, i)),
                  pl.BlockSpec((nh, 1), lambda i: (0, 0)),
                  pl.BlockSpec((nh, 1), lambda i: (0, 0))],
        out_specs=[pl.BlockSpec((nh, tl), lambda i: (0, i)),
                   pl.BlockSpec((nh, tl), lambda i: (0, i))],
        out_shape=[jax.ShapeDtypeStruct((nh, T), F32), jax.ShapeDtypeStruct((nh, T), F32)],
        compiler_params=_params(("arbitrary",)),
        name="gdn_gates",
    )(ba_t, a_log.reshape(nh, 1), dt_bias.reshape(nh, 1))


def _bdot(a, b):
    return lax.dot_general(a, b, (((2,), (1,)), ((0,), (0,))), preferred_element_type=F32)


def _bdot_nt(a, b):
    return lax.dot_general(a, b, (((2,), (2,)), ((0,), (0,))), preferred_element_type=F32)


def _gdn_kernel(q_ref, k_ref, v_ref, z_ref, gcol_ref, bcol_ref, grow_ref, ong_ref, o_ref,
                s_ref, wn_ref, bn_ref, qn_ref, on_ref, egl_ref):
    p = pl.program_id(0)
    c = pl.program_id(1)
    C = GDN_CHUNK
    D = HEAD_DIM
    nc = wn_ref.shape[1]
    rep = s_ref.shape[0]

    @pl.when(c == 0)
    def _():
        s_ref[...] = jnp.zeros_like(s_ref)

    ri = lax.broadcasted_iota(jnp.int32, (C, C), 0)
    ci = lax.broadcasted_iota(jnp.int32, (C, C), 1)
    causal = ri >= ci
    strict = (ri > ci)[None]
    eye = jnp.where(ri == ci, 1.0, 0.0)[None]
    lane = lax.broadcasted_iota(jnp.int32, gcol_ref.shape, 1)

    k3 = k_ref[...].reshape(nc, C, D)
    q3 = q_ref[...].reshape(nc, C, D) * (D ** -0.5)
    k16 = k3.astype(BF16)
    kk = _bdot_nt(k16, k16)
    qk = _bdot_nt(q3.astype(BF16), k16)
    for hp in range(rep):
        h = p * rep + hp
        gc3 = jnp.sum(jnp.where(lane == h, gcol_ref[...], 0.0), axis=1, keepdims=True).reshape(nc, C, 1)
        bt3 = jnp.sum(jnp.where(lane == h, bcol_ref[...], 0.0), axis=1, keepdims=True).reshape(nc, C, 1)
        decay = jnp.stack([
            jnp.where(causal, jnp.exp(jnp.where(causal, gc3[n] - grow_ref[hp, n:n + 1, :], 0.0)), 0.0)
            for n in range(nc)])
        lmat = jnp.where(strict, kk * bt3 * decay, 0.0)
        inv = eye - lmat
        m = lmat
        for _ in range(5):
            m16 = m.astype(BF16)
            m = _bdot(m16, m16)
            inv = inv + _bdot(inv.astype(BF16), m.astype(BF16))
        inv16 = inv.astype(BF16)
        eg3 = jnp.exp(gc3)
        v3 = v_ref[:, hp * D:(hp + 1) * D].reshape(nc, C, D)
        u16 = _bdot(inv16, (v3 * bt3).astype(BF16)).astype(BF16)
        w16 = _bdot(inv16, (k3 * (bt3 * eg3)).astype(BF16)).astype(BF16)
        attn16 = jnp.where(causal[None], qk * decay, 0.0).astype(BF16)
        gl3 = gc3[:, C - 1:C, :]
        kd3 = k3 * jnp.exp(gl3 - gc3)
        kdt16 = jnp.stack([kd3[n].T for n in range(nc)]).astype(BF16)
        wn_ref[hp] = _bdot(kdt16, w16).astype(wn_ref.dtype)
        bn_ref[hp] = _bdot(kdt16, u16)
        qn_ref[hp] = (q3 * eg3 - _bdot(attn16, w16)).astype(qn_ref.dtype)
        on_ref[hp] = _bdot(attn16, u16)
        egl_ref[hp] = jnp.broadcast_to(jnp.exp(gl3), egl_ref.shape[1:])

    ong = ong_ref[...]
    states = [s_ref[hp] for hp in range(rep)]
    for n in range(nc):
        sl = slice(n * C, (n + 1) * C)
        for hp in range(rep):
            s16 = states[hp].astype(BF16)
            o = _dot(qn_ref[hp, n], s16) + on_ref[hp, n]
            states[hp] = states[hp] * egl_ref[hp, n, :1, :] - _dot(wn_ref[hp, n], s16) + bn_ref[hp, n]
            ms = jnp.mean(o * o, axis=-1, keepdims=True)
            on = o * lax.rsqrt(ms + RMS_EPS) * ong
            z = z_ref[sl, hp * D:(hp + 1) * D]
            o_ref[sl, hp * D:(hp + 1) * D] = (on * (z * _sigmoid(z))).astype(o_ref.dtype)
    for hp in range(rep):
        s_ref[hp] = states[hp]


def gdn_delta_rule(qkv, proj, gc_col, beta_col, gc_row, out_norm_g, *, tb=512):
    T = qkv.shape[0]
    H = GDN_V_HEADS
    rep = GDN_V_HEADS // GDN_QK_HEADS
    kq = GDN_QK_HEADS
    C = GDN_CHUNK
    D = HEAD_DIM
    assert T % tb == 0 and tb % C == 0
    nc = tb // C
    v_blk0 = 2 * kq // rep
    z_blk0 = (2 * kq + H) // rep
    return pl.pallas_call(
        _gdn_kernel,
        grid=(kq, T // tb),
        in_specs=[pl.BlockSpec((tb, D), lambda p, c: (c, p)),
                  pl.BlockSpec((tb, D), lambda p, c: (c, kq + p)),
                  pl.BlockSpec((tb, rep * D), lambda p, c: (c, v_blk0 + p)),
                  pl.BlockSpec((tb, rep * D), lambda p, c: (c, z_blk0 + p)),
                  pl.BlockSpec((tb, H), lambda p, c: (c, 0)),
                  pl.BlockSpec((tb, H), lambda p, c: (c, 0)),
                  pl.BlockSpec((rep, nc, C), lambda p, c: (p, c, 0)),
                  pl.BlockSpec((1, D), lambda p, c: (0, 0))],
        out_specs=pl.BlockSpec((tb, rep * D), lambda p, c: (c, p)),
        out_shape=jax.ShapeDtypeStruct((T, H * D), BF16),
        scratch_shapes=[pltpu.VMEM((rep, D, D), F32),
                        pltpu.VMEM((rep, nc, D, D), BF16),
                        pltpu.VMEM((rep, nc, D, D), F32),
                        pltpu.VMEM((rep, nc, C, D), BF16),
                        pltpu.VMEM((rep, nc, C, D), F32),
                        pltpu.VMEM((rep, nc, SUBLANES, D), F32)],
        compiler_params=_params(("arbitrary", "arbitrary")),
        name="gdn_delta_rule",
    )(qkv, qkv, qkv, proj, gc_col, beta_col, gc_row.reshape(H, T // C, C), out_norm_g.reshape(1, D))


def _cmp_kernel(x_ref, pos_ref, w1_ref, w2_ref, o_ref):
    x = x_ref[...]
    half = x.shape[1]
    top = (x + pos_ref[0:1, :]).astype(BF16)
    bot = (x + pos_ref[1:2, :]).astype(BF16)
    a = _dot(top, w1_ref[:half, :])
    b = _dot(bot, w1_ref[half:, :])
    pre = a + pltpu.roll(b, x.shape[0] - 1, 0)
    hid = pre * _sigmoid(pre)
    o_ref[...] = _dot(hid.astype(BF16), w2_ref[...]).astype(o_ref.dtype)


def compress_kv(kvc, pos, w1, w2):
    n8, nseg, half = kvc.shape
    G = n8 // 2
    hid = w1.shape[2]
    return pl.pallas_call(
        _cmp_kernel,
        grid=(n8,),
        in_specs=[pl.BlockSpec((None, nseg, half), lambda i: (i, 0, 0)),
                  pl.BlockSpec((2, half), lambda i: (0, 0)),
                  pl.BlockSpec((None, 2 * half, hid), lambda i: (i // G, 0, 0)),
                  pl.BlockSpec((None, hid, HEAD_DIM), lambda i: (i // G, 0, 0))],
        out_specs=pl.BlockSpec((None, nseg, HEAD_DIM), lambda i: (i, 0, 0)),
        out_shape=jax.ShapeDtypeStruct((n8, nseg, HEAD_DIM), BF16),
        compiler_params=_params(("arbitrary",)),
        name="nsa_compress",
    )(kvc, pos, w1, w2)


def _softmax_rows(s, valid):
    m = jnp.max(jnp.where(valid, s, NEG_INF), axis=-1, keepdims=True)
    e = jnp.where(valid, jnp.exp(jnp.where(valid, s - m, 0.0)), 0.0)
    den = jnp.sum(e, axis=-1, keepdims=True)
    return e / jnp.where(den > 0.0, den, 1.0)


def _nsa_kernel(q_ref, gate_ref, slope_ref, kc_ref, vc_ref, ks_ref, vs_ref, kw_ref, vw_ref,
                wsel_ref, exp_ref, o_ref, *, n_sel, tk):
    qb = pl.program_id(1)
    start = qb * Q_BLOCK
    HG = NSA_HPG
    R = HG * Q_BLOCK
    D = HEAD_DIM
    q = jnp.concatenate([q_ref[:, hh * D:(hh + 1) * D] for hh in range(HG)], axis=0)
    slope = slope_ref[...]
    tq = start + lax.broadcasted_iota(jnp.int32, (Q_BLOCK, 1), 0)
    tq_f = tq.astype(F32)

    ncb = kc_ref.shape[0]
    cidx = lax.broadcasted_iota(jnp.int32, (1, ncb), 1)
    centre = cidx.astype(F32) * CMP_STRIDE + (CMP_BLOCK - 1) / 2
    valid_c = ((cidx * CMP_STRIDE + (CMP_BLOCK - 1)) <= tq)[None]
    s_c = _dot_nt(q, kc_ref[...]).reshape(HG, Q_BLOCK, ncb) - slope * (tq_f - centre)[None]
    p_c = _softmax_rows(s_c, valid_c)
    o_cmp = _dot(p_c.reshape(R, ncb).astype(BF16), vc_ref[...])
    imp = jnp.sum(p_c, axis=0)

    nsb = wsel_ref.shape[1]
    wsel = wsel_ref[...]
    i1 = imp.astype(BF16)
    r1 = imp - i1.astype(F32)
    i2 = r1.astype(BF16)
    i3 = (r1 - i2.astype(F32)).astype(BF16)
    imp_slc = _dot(i1, wsel) + (_dot(i2, wsel) + _dot(i3, wsel))
    blk = lax.broadcasted_iota(jnp.int32, (Q_BLOCK, nsb), 1)
    cur = tq // SEL_BLOCK
    forced = (blk == 0) | (blk == cur) | (blk == cur - 1)
    visible = blk * SEL_BLOCK <= tq
    score = jnp.where(visible, jnp.where(forced, FORCE_SCORE, imp_slc), NEG_INF)
    selected = jnp.zeros((Q_BLOCK, nsb), F32)
    for _ in range(n_sel):
        mx = jnp.max(score, axis=-1, keepdims=True)
        first = jnp.min(jnp.where(score == mx, blk, nsb), axis=-1, keepdims=True)
        pick = blk == first
        selected = jnp.where(pick, 1.0, selected)
        score = jnp.where(pick, -jnp.inf, score)

    bpt = tk // SEL_BLOCK
    n_tiles = (start + Q_BLOCK - 1) // tk + 1
    expand = exp_ref[...]

    def sel_tile(kt, carry):
        m_run, l_run, acc = carry
        k0 = pl.multiple_of(kt * tk, tk)
        shifted = pltpu.roll(selected, nsb - kt * bpt, 1)
        sel_keys = _dot(shifted[:, :LANES].astype(BF16), expand)
        kpos = k0 + lax.broadcasted_iota(jnp.int32, (1, tk), 1)
        ok = ((sel_keys > 0.5) & (kpos <= tq))[None]
        dist = (tq_f - kpos.astype(F32))[None]
        s = _dot_nt(q, ks_ref[pl.ds(k0, tk), :]).reshape(HG, Q_BLOCK, tk) - slope * dist
        s = jnp.where(ok, s, NEG_INF)
        m_new = jnp.maximum(m_run, jnp.max(s, axis=-1, keepdims=True))
        p = jnp.where(ok, jnp.exp(s - m_new), 0.0)
        alpha = jnp.exp(m_run - m_new)
        l_new = alpha * l_run + jnp.sum(p, axis=-1, keepdims=True)
        pv = _dot(p.reshape(R, tk).astype(BF16), vs_ref[pl.ds(k0, tk), :])
        acc = alpha * acc + pv.reshape(HG, Q_BLOCK, D)
        return m_new, l_new, acc

    init = (jnp.full((HG, Q_BLOCK, 1), NEG_INF, F32), jnp.zeros((HG, Q_BLOCK, 1), F32),
            jnp.zeros((HG, Q_BLOCK, D), F32))
    _, l_s, acc_s = lax.fori_loop(0, n_tiles, sel_tile, init)
    o_slc = acc_s / l_s

    wk = WINDOW + Q_BLOCK
    w0 = pl.multiple_of(jnp.maximum(start - WINDOW, 0), Q_BLOCK)
    kpos_w = w0 + lax.broadcasted_iota(jnp.int32, (1, wk), 1)
    dist_w = tq - kpos_w
    valid_w = ((dist_w >= 0) & (dist_w < WINDOW))[None]
    s_w = _dot_nt(q, kw_ref[pl.ds(w0, wk), :]).reshape(HG, Q_BLOCK, wk) - slope * dist_w.astype(F32)[None]
    p_w = _softmax_rows(s_w, valid_w)
    o_win = _dot(p_w.reshape(R, wk).astype(BF16), vw_ref[pl.ds(w0, wk), :])

    gates = _sigmoid(gate_ref[...])
    for hh in range(HG):
        rows = slice(hh * Q_BLOCK, (hh + 1) * Q_BLOCK)
        out = (gates[:, hh:hh + 1] * o_cmp[rows]
               + gates[:, HG + hh:HG + hh + 1] * o_slc[hh]
               + gates[:, 2 * HG + hh:2 * HG + hh + 1] * o_win[rows])
        o_ref[:, hh * D:(hh + 1) * D] = out.astype(o_ref.dtype)


def nsa_attention(q, gate_logits, slopes, k_cmp, v_cmp, k_slc, v_slc, k_win, v_win, *, tk=512):
    T = q.shape[0]
    G, HG, D = NSA_GROUPS, NSA_HPG, HEAD_DIM
    nqb = T // Q_BLOCK
    ncb = k_cmp.shape[1]
    nsb = T // SEL_BLOCK
    n_sel = min(SEL_COUNT, nsb)
    assert T % tk == 0 and tk % SEL_BLOCK == 0 and T >= WINDOW + Q_BLOCK and tk // SEL_BLOCK <= LANES
    nsb_pad = -(-nsb // LANES) * LANES
    cc = jnp.arange(ncb)[:, None]
    jj = jnp.arange(nsb_pad)[None, :]
    off = cc - (SEL_RATIO * jj - 1)
    wsel = jnp.zeros((ncb, nsb_pad), F32)
    for o, wt in enumerate(SEL_OVERLAP_W):
        wsel = jnp.where((off == o) & (jj < nsb), wt, wsel)
    wsel = wsel.astype(BF16)
    expand = (jnp.arange(LANES)[:, None] == (jnp.arange(tk)[None, :] // SEL_BLOCK)).astype(BF16)
    full = lambda rows: pl.BlockSpec((None, rows, D), lambda g, b: (g, 0, 0))
    return pl.pallas_call(
        functools.partial(_nsa_kernel, n_sel=n_sel, tk=tk),
        grid=(G, nqb),
        in_specs=[pl.BlockSpec((Q_BLOCK, HG * D), lambda g, b: (b, g)),
                  pl.BlockSpec((None, Q_BLOCK, 3 * HG), lambda g, b: (g, b, 0)),
                  pl.BlockSpec((None, HG, 1, 1), lambda g, b: (g, 0, 0, 0)),
                  full(ncb), full(ncb), full(T), full(T), full(T), full(T),
                  pl.BlockSpec((ncb, nsb_pad), lambda g, b: (0, 0)),
                  pl.BlockSpec((LANES, tk), lambda g, b: (0, 0))],
        out_specs=pl.BlockSpec((Q_BLOCK, HG * D), lambda g, b: (b, g)),
        out_shape=jax.ShapeDtypeStruct((T, NSA_HEADS * D), BF16),
        compiler_params=_params(("arbitrary", "arbitrary")),
        name="nsa_attention",
    )(q, gate_logits, slopes.reshape(G, HG, 1, 1), k_cmp, v_cmp, k_slc, v_slc, k_win, v_win, wsel, expand)


def _pad_cols(w, n):
    return jnp.pad(w, ((0, 0), (0, n - w.shape[1])))


def kernel(x, mixer_norm_g, ffn_norm_g, ffn_w_gate_up, ffn_w_down, gdn_w_in, gdn_conv_w, gdn_a_log, gdn_dt_bias, gdn_out_norm_g, gdn_w_out, kv_norm_g, kv_w, cmp_pos, cmp_w1_k, cmp_w2_k, cmp_w1_v, cmp_w2_v, nsa_w_in, nsa_w_out, final_norm_g):
    B, T, Dm = x.shape
    assert B == 1
    h = x.reshape(T, Dm)
    D = HEAD_DIM
    qk_dim = GDN_QK_HEADS * D
    v_dim = GDN_V_HEADS * D
    conv_dim = 2 * qk_dim + v_dim
    main = conv_dim + v_dim

    w_in = gdn_w_in[0]
    proj = norm_matmul(h, mixer_norm_g[0], w_in[:, :main].astype(BF16), out_dtype=F32)
    ba = norm_matmul(h, mixer_norm_g[0], _pad_cols(w_in[:, main:], LANES).astype(BF16), out_dtype=F32)
    beta_t, gc_t = gdn_gates(ba[:, :2 * GDN_V_HEADS].T, gdn_a_log[0], gdn_dt_bias[0])
    qkv = gdn_conv(proj, gdn_conv_w[0], n_qk_cols=2 * qk_dim)
    o = gdn_delta_rule(qkv, proj, gc_t.T, beta_t.T, gc_t, gdn_out_norm_g[0])
    h = matmul_residual(o, gdn_w_out[0].astype(BF16), h)
    act = norm_swiglu(h, ffn_norm_g[0], ffn_w_gate_up[0].astype(BF16))
    h = matmul_residual(act, ffn_w_down[0].astype(BF16), h)

    G = NSA_GROUPS
    n_cmp_cols = 2 * G * D
    kv_w16 = kv_w.astype(BF16)
    kvc = norm_matmul(h, kv_norm_g, kv_w16[:, :n_cmp_cols], out_dtype=F32, split=True)
    kvr = norm_matmul(h, kv_norm_g, kv_w16[:, n_cmp_cols:], out_dtype=BF16, split=True)
    nseg = T // CMP_STRIDE
    pos2 = cmp_pos.reshape(2, CMP_STRIDE * D)
    w1 = jnp.stack([cmp_w1_k, cmp_w1_v]).astype(BF16)
    w2 = jnp.stack([cmp_w2_k, cmp_w2_v]).astype(BF16)
    kv_cmp = compress_kv(kvc.reshape(2 * G, nseg, CMP_STRIDE * D), pos2, w1, w2)

    w_nsa = nsa_w_in[0]
    q_dim = NSA_HEADS * D
    q = norm_matmul(h, mixer_norm_g[1], w_nsa[:, :q_dim].astype(BF16), out_dtype=BF16, scale=D ** -0.5)
    gl = norm_matmul(h, mixer_norm_g[1], _pad_cols(w_nsa[:, q_dim:], LANES).astype(BF16), out_dtype=F32)
    gl = gl[:, :3 * NSA_HEADS].reshape(T, 3, G, NSA_HPG).transpose(2, 0, 1, 3).reshape(G, T, 3 * NSA_HPG)
    slopes = 2.0 ** (-8.0 * jnp.arange(1, NSA_HEADS + 1, dtype=F32) / NSA_HEADS)
    o = nsa_attention(q, gl, slopes, kv_cmp[:G], kv_cmp[G:], kvr[:G], kvr[G:2 * G], kvr[2 * G:3 * G], kvr[3 * G:])
    h = matmul_residual(o, nsa_w_out[0].astype(BF16), h)
    act = norm_swiglu(h, ffn_norm_g[1], ffn_w_gate_up[1].astype(BF16))
    h = matmul_residual(act, ffn_w_down[1].astype(BF16), h)
    return rmsnorm(h, final_norm_g).reshape(B, T, Dm)
```

```python
import functools

import jax
import jax.numpy as jnp
from jax import lax
from jax.experimental import pallas as pl
from jax.experimental.pallas import tpu as pltpu

F32 = jnp.float32
BF16 = jnp.bfloat16

RMS_EPS = 1e-6
L2_EPS = 1e-6
NEG_INF = -1e30
FORCE_SCORE = 1e4

GDN_QK_HEADS = 16
GDN_V_HEADS = 32
HEAD_DIM = 128
GDN_CONV = 4
GDN_CHUNK = 64

NSA_HEADS = 16
NSA_GROUPS = 4
NSA_HPG = NSA_HEADS // NSA_GROUPS
CMP_BLOCK = 32
CMP_STRIDE = 16
SEL_BLOCK = 64
SEL_COUNT = 16
WINDOW = 512
Q_BLOCK = 128
SEL_RATIO = SEL_BLOCK // CMP_STRIDE
SEL_OVERLAP_W = (1.0, 2.0, 2.0, 2.0, 1.0)
SEL_HEAD_SPLIT = 1

V7X_VMEM_LIMIT_BYTES = 56 * 1024 * 1024
LANES = 128
SUBLANES = 8


def _params(sem):
    return pltpu.CompilerParams(dimension_semantics=sem, vmem_limit_bytes=V7X_VMEM_LIMIT_BYTES)


def _sigmoid(x):
    return 1.0 / (1.0 + jnp.exp(-x))


def _dot(a, b):
    return jnp.dot(a, b, preferred_element_type=F32)


def _dot_nt(a, b):
    return lax.dot_general(a, b, (((1,), (1,)), ((), ())), preferred_element_type=F32)


def _split_bf16(a):
    hi = a.astype(BF16)
    lo = (a - hi.astype(F32)).astype(BF16)
    return hi, lo


def _dot_hi(a, b):
    ah, al = _split_bf16(a)
    bh, bl = _split_bf16(b)
    return _dot(ah, bh) + (_dot(ah, bl) + _dot(al, bh))


def _norm_rows(x_ref, g_ref, xn_ref):
    x = x_ref[...]
    ms = jnp.mean(x * x, axis=-1, keepdims=True)
    xn_ref[...] = (x * lax.rsqrt(ms + RMS_EPS) * g_ref[...]).astype(xn_ref.dtype)


def _norm_mm_kernel(x_ref, g_ref, w_ref, o_ref, xn_ref, *, scale, split):
    @pl.when(pl.program_id(1) == 0)
    def _():
        _norm_rows(x_ref, g_ref, xn_ref)

    acc = _dot(xn_ref[...], w_ref[...])
    if scale is not None:
        acc = acc * scale
    if split:
        for c in range(o_ref.shape[0]):
            o_ref[c] = acc[:, c * LANES:(c + 1) * LANES].astype(o_ref.dtype)
    else:
        o_ref[...] = acc.astype(o_ref.dtype)


def norm_matmul(x, g, w, *, out_dtype, scale=None, split=False, tm=1024, tn=512):
    T, D = x.shape
    N = w.shape[1]
    tn = min(tn, N)
    assert T % tm == 0 and N % tn == 0 and tn % LANES == 0
    if split:
        out_shape = jax.ShapeDtypeStruct((N // LANES, T, LANES), out_dtype)
        out_spec = pl.BlockSpec((tn // LANES, tm, LANES), lambda i, j: (j, i, 0))
    else:
        out_shape = jax.ShapeDtypeStruct((T, N), out_dtype)
        out_spec = pl.BlockSpec((tm, tn), lambda i, j: (i, j))
    return pl.pallas_call(
        functools.partial(_norm_mm_kernel, scale=scale, split=split),
        grid=(T // tm, N // tn),
        in_specs=[pl.BlockSpec((tm, D), lambda i, j: (i, 0)),
                  pl.BlockSpec((1, D), lambda i, j: (0, 0)),
                  pl.BlockSpec((D, tn), lambda i, j: (0, j))],
        out_specs=out_spec,
        out_shape=out_shape,
        scratch_shapes=[pltpu.VMEM((tm, D), BF16)],
        compiler_params=_params(("arbitrary", "arbitrary")),
        name="norm_matmul",
    )(x, g.reshape(1, D), w)


def _norm_swiglu_kernel(x_ref, g_ref, wg_ref, wu_ref, o_ref, xn_ref):
    @pl.when(pl.program_id(1) == 0)
    def _():
        _norm_rows(x_ref, g_ref, xn_ref)

    xn = xn_ref[...]
    gate = _dot(xn, wg_ref[...])
    up = _dot(xn, wu_ref[...])
    o_ref[...] = (gate * _sigmoid(gate) * up).astype(o_ref.dtype)


def norm_swiglu(x, g, w_gate_up, *, tm=1024, tn=512):
    T, D = x.shape
    F = w_gate_up.shape[1] // 2
    assert T % tm == 0 and F % tn == 0
    nj = F // tn
    return pl.pallas_call(
        _norm_swiglu_kernel,
        grid=(T // tm, nj),
        in_specs=[pl.BlockSpec((tm, D), lambda i, j: (i, 0)),
                  pl.BlockSpec((1, D), lambda i, j: (0, 0)),
                  pl.BlockSpec((D, tn), lambda i, j: (0, j)),
                  pl.BlockSpec((D, tn), lambda i, j: (0, j + nj))],
        out_specs=pl.BlockSpec((tm, tn), lambda i, j: (i, j)),
        out_shape=jax.ShapeDtypeStruct((T, F), BF16),
        scratch_shapes=[pltpu.VMEM((tm, D), BF16)],
        compiler_params=_params(("arbitrary", "arbitrary")),
        name="norm_swiglu",
    )(x, g.reshape(1, D), w_gate_up, w_gate_up)


def _mm_res_kernel(a_ref, w_ref, r_ref, o_ref):
    o_ref[...] = r_ref[...] + _dot(a_ref[...], w_ref[...])


def matmul_residual(a, w, res, *, tm=512, tn=512):
    T, K = a.shape
    N = w.shape[1]
    assert T % tm == 0 and N % tn == 0
    return pl.pallas_call(
        _mm_res_kernel,
        grid=(T // tm, N // tn),
        in_specs=[pl.BlockSpec((tm, K), lambda i, j: (i, 0)),
                  pl.BlockSpec((K, tn), lambda i, j: (0, j)),
                  pl.BlockSpec((tm, tn), lambda i, j: (i, j))],
        out_specs=pl.BlockSpec((tm, tn), lambda i, j: (i, j)),
        out_shape=jax.ShapeDtypeStruct((T, N), F32),
        compiler_params=_params(("arbitrary", "arbitrary")),
        name="matmul_residual",
    )(a, w, res)


def _rmsnorm_kernel(x_ref, g_ref, o_ref):
    _norm_rows(x_ref, g_ref, o_ref)


def rmsnorm(x, g, *, tm=512):
    T, D = x.shape
    return pl.pallas_call(
        _rmsnorm_kernel,
        grid=(T // tm,),
        in_specs=[pl.BlockSpec((tm, D), lambda i: (i, 0)),
                  pl.BlockSpec((1, D), lambda i: (0, 0))],
        out_specs=pl.BlockSpec((tm, D), lambda i: (i, 0)),
        out_shape=jax.ShapeDtypeStruct((T, D), x.dtype),
        compiler_params=_params(("arbitrary",)),
        name="final_rmsnorm",
    )(x, g.reshape(1, D))


def _conv_kernel(x_ref, halo_ref, w_ref, o_ref, *, n_qk_blocks, heads_per_block):
    i = pl.program_id(0)
    j = pl.program_id(1)
    x = x_ref[...]
    halo = jnp.where(i == 0, 0.0, halo_ref[...])
    w = w_ref[...]
    top = x[:SUBLANES]
    row8 = lax.broadcasted_iota(jnp.int32, top.shape, 0)
    y = w[GDN_CONV - 1:GDN_CONV] * x
    y_top = w[GDN_CONV - 1:GDN_CONV] * top
    for s in range(1, GDN_CONV):
        wk = w[GDN_CONV - 1 - s:GDN_CONV - s]
        xs = pltpu.roll(x, s, 0)
        y = y + wk * xs
        hs = pltpu.roll(halo, s, 0)
        y_top = y_top + wk * jnp.where(row8 < s, hs, xs[:SUBLANES])
    is_qk = j < n_qk_blocks

    def finish(v):
        v = v * _sigmoid(v)
        outs = []
        for hh in range(heads_per_block):
            vh = v[:, hh * HEAD_DIM:(hh + 1) * HEAD_DIM]
            inv = lax.rsqrt(jnp.sum(vh * vh, axis=-1, keepdims=True) + L2_EPS)
            outs.append(vh * jnp.where(is_qk, inv, 1.0))
        return jnp.concatenate(outs, axis=-1) if len(outs) > 1 else outs[0]

    o_ref[...] = finish(y).astype(o_ref.dtype)
    o_ref[:SUBLANES, :] = finish(y_top).astype(o_ref.dtype)


def gdn_conv(proj, conv_w, *, n_qk_cols, tb=512, cb=512):
    T = proj.shape[0]
    C = conv_w.shape[1]
    assert T % tb == 0 and C % cb == 0 and n_qk_cols % cb == 0 and tb % SUBLANES == 0
    hb = tb // SUBLANES
    return pl.pallas_call(
        functools.partial(_conv_kernel, n_qk_blocks=n_qk_cols // cb, heads_per_block=cb // HEAD_DIM),
        grid=(T // tb, C // cb),
        in_specs=[pl.BlockSpec((tb, cb), lambda i, j: (i, j)),
                  pl.BlockSpec((SUBLANES, cb), lambda i, j: (jnp.maximum(i * hb - 1, 0), j)),
                  pl.BlockSpec((GDN_CONV, cb), lambda i, j: (0, j))],
        out_specs=pl.BlockSpec((tb, cb), lambda i, j: (i, j)),
        out_shape=jax.ShapeDtypeStruct((T, C), F32),
        compiler_params=_params(("arbitrary", "arbitrary")),
        name="gdn_conv",
    )(proj, proj, conv_w)


def _gate_kernel(ba_ref, alog_ref, dtb_ref, beta_ref, gc_ref):
    nh = beta_ref.shape[0]
    b_raw = ba_ref[:nh, :]
    a_raw = ba_ref[nh:2 * nh, :]
    beta_ref[...] = _sigmoid(b_raw)
    xs = a_raw + dtb_ref[...]
    softplus = jnp.maximum(xs, 0.0) + jnp.log(1.0 + jnp.exp(-jnp.abs(xs)))
    g = -jnp.exp(alog_ref[...]) * softplus
    pos = lax.broadcasted_iota(jnp.int32, g.shape, 1) % GDN_CHUNK
    s = 1
    while s < GDN_CHUNK:
        g = g + jnp.where(pos >= s, pltpu.roll(g, s, 1), 0.0)
        s *= 2
    gc_ref[...] = g


def gdn_gates(ba_t, a_log, dt_bias, *, tl=2048):
    nh2, T = ba_t.shape
    nh = nh2 // 2
    tl = min(tl, T)
    assert T % tl == 0 and tl % GDN_CHUNK == 0
    return pl.pallas_call(
        _gate_kernel,
        grid=(T // tl,),
        in_specs=[pl.BlockSpec((nh2, tl), lambda i: (0, i)),
                  pl.BlockSpec((nh, 1), lambda i: (0, 0)),
                  pl.BlockSpec((nh, 1), lambda i: (0, 0))],
        out_specs=[pl.BlockSpec((nh, tl), lambda i: (0, i)),
                   pl.BlockSpec((nh, tl), lambda i: (0, i))],
        out_shape=[jax.ShapeDtypeStruct((nh, T), F32), jax.ShapeDtypeStruct((nh, T), F32)],
        compiler_params=_params(("arbitrary",)),
        name="gdn_gates",
    )(ba_t, a_log.reshape(nh, 1), dt_bias.reshape(nh, 1))


def _bdot(a, b):
    return lax.dot_general(a, b, (((2,), (1,)), ((0,), (0,))), preferred_element_type=F32)


def _bdot_nt(a, b):
    return lax.dot_general(a, b, (((2,), (2,)), ((0,), (0,))), preferred_element_type=F32)


def _gdn_kernel(q_ref, k_ref, v_ref, z_ref, gcol_ref, bcol_ref, grow_ref, ong_ref, o_ref,
                s_ref, wn_ref, bn_ref, qn_ref, on_ref, egl_ref):
    p = pl.program_id(0)
    c = pl.program_id(1)
    C = GDN_CHUNK
    D = HEAD_DIM
    nc = wn_ref.shape[1]
    rep = s_ref.shape[0]

    @pl.when(c == 0)
    def _():
        s_ref[...] = jnp.zeros_like(s_ref)

    ri = lax.broadcasted_iota(jnp.int32, (C, C), 0)
    ci = lax.broadcasted_iota(jnp.int32, (C, C), 1)
    causal = ri >= ci
    strict = (ri > ci)[None]
    eye = jnp.where(ri == ci, 1.0, 0.0)[None]
    lane = lax.broadcasted_iota(jnp.int32, gcol_ref.shape, 1)

    k3 = k_ref[...].reshape(nc, C, D)
    q3 = q_ref[...].reshape(nc, C, D) * (D ** -0.5)
    k16 = k3.astype(BF16)
    kk = _bdot_nt(k16, k16)
    qk = _bdot_nt(q3.astype(BF16), k16)
    for hp in range(rep):
        h = p * rep + hp
        gc3 = jnp.sum(jnp.where(lane == h, gcol_ref[...], 0.0), axis=1, keepdims=True).reshape(nc, C, 1)
        bt3 = jnp.sum(jnp.where(lane == h, bcol_ref[...], 0.0), axis=1, keepdims=True).reshape(nc, C, 1)
        decay = jnp.stack([
            jnp.where(causal, jnp.exp(jnp.where(causal, gc3[n] - grow_ref[hp, n:n + 1, :], 0.0)), 0.0)
            for n in range(nc)])
        lmat = jnp.where(strict, kk * bt3 * decay, 0.0)
        inv = eye - lmat
        m = lmat
        for _ in range(5):
            m16 = m.astype(BF16)
            m = _bdot(m16, m16)
            inv = inv + _bdot(inv.astype(BF16), m.astype(BF16))
        inv16 = inv.astype(BF16)
        eg3 = jnp.exp(gc3)
        v3 = v_ref[:, hp * D:(hp + 1) * D].reshape(nc, C, D)
        u16 = _bdot(inv16, (v3 * bt3).astype(BF16)).astype(BF16)
        w16 = _bdot(inv16, (k3 * (bt3 * eg3)).astype(BF16)).astype(BF16)
        attn16 = jnp.where(causal[None], qk * decay, 0.0).astype(BF16)
        gl3 = gc3[:, C - 1:C, :]
        kd3 = k3 * jnp.exp(gl3 - gc3)
        kdt16 = jnp.stack([kd3[n].T for n in range(nc)]).astype(BF16)
        wn_ref[hp] = _bdot(kdt16, w16).astype(wn_ref.dtype)
        bn_ref[hp] = _bdot(kdt16, u16)
        qn_ref[hp] = (q3 * eg3 - _bdot(attn16, w16)).astype(qn_ref.dtype)
        on_ref[hp] = _bdot(attn16, u16)
        egl_ref[hp] = jnp.broadcast_to(jnp.exp(gl3), egl_ref.shape[1:])

    ong = ong_ref[...]
    states = [s_ref[hp] for hp in range(rep)]
    for n in range(nc):
        sl = slice(n * C, (n + 1) * C)
        for hp in range(rep):
            s16 = states[hp].astype(BF16)
            o = _dot(qn_ref[hp, n], s16) + on_ref[hp, n]
            states[hp] = states[hp] * egl_ref[hp, n, :1, :] - _dot(wn_ref[hp, n], s16) + bn_ref[hp, n]
            ms = jnp.mean(o * o, axis=-1, keepdims=True)
            on = o * lax.rsqrt(ms + RMS_EPS) * ong
            z = z_ref[sl, hp * D:(hp + 1) * D]
            o_ref[sl, hp * D:(hp + 1) * D] = (on * (z * _sigmoid(z))).astype(o_ref.dtype)
    for hp in range(rep):
        s_ref[hp] = states[hp]


def gdn_delta_rule(qkv, proj, gc_col, beta_col, gc_row, out_norm_g, *, tb=512):
    T = qkv.shape[0]
    H = GDN_V_HEADS
    rep = GDN_V_HEADS // GDN_QK_HEADS
    kq = GDN_QK_HEADS
    C = GDN_CHUNK
    D = HEAD_DIM
    assert T % tb == 0 and tb % C == 0
    nc = tb // C
    v_blk0 = 2 * kq // rep
    z_blk0 = (2 * kq + H) // rep
    return pl.pallas_call(
        _gdn_kernel,
        grid=(kq, T // tb),
        in_specs=[pl.BlockSpec((tb, D), lambda p, c: (c, p)),
                  pl.BlockSpec((tb, D), lambda p, c: (c, kq + p)),
                  pl.BlockSpec((tb, rep * D), lambda p, c: (c, v_blk0 + p)),
                  pl.BlockSpec((tb, rep * D), lambda p, c: (c, z_blk0 + p)),
                  pl.BlockSpec((tb, H), lambda p, c: (c, 0)),
                  pl.BlockSpec((tb, H), lambda p, c: (c, 0)),
                  pl.BlockSpec((rep, nc, C), lambda p, c: (p, c, 0)),
                  pl.BlockSpec((1, D), lambda p, c: (0, 0))],
        out_specs=pl.BlockSpec((tb, rep * D), lambda p, c: (c, p)),
        out_shape=jax.ShapeDtypeStruct((T, H * D), BF16),
        scratch_shapes=[pltpu.VMEM((rep, D, D), F32),
                        pltpu.VMEM((rep, nc, D, D), BF16),
                        pltpu.VMEM((rep, nc, D, D), F32),
                        pltpu.VMEM((rep, nc, C, D), BF16),
                        pltpu.VMEM((rep, nc, C, D), F32),
                        pltpu.VMEM((rep, nc, SUBLANES, D), F32)],
        compiler_params=_params(("arbitrary", "arbitrary")),
        name="gdn_delta_rule",
    )(qkv, qkv, qkv, proj, gc_col, beta_col, gc_row.reshape(H, T // C, C), out_norm_g.reshape(1, D))


def _cmp_kernel(x_ref, pos_ref, w1_ref, w2_ref, o_ref):
    x = x_ref[...]
    half = x.shape[1]
    top = (x + pos_ref[0:1, :]).astype(BF16)
    bot = (x + pos_ref[1:2, :]).astype(BF16)
    a = _dot(top, w1_ref[:half, :])
    b = _dot(bot, w1_ref[half:, :])
    pre = a + pltpu.roll(b, x.shape[0] - 1, 0)
    hid = pre * _sigmoid(pre)
    o_ref[...] = _dot(hid.astype(BF16), w2_ref[...]).astype(o_ref.dtype)


def compress_kv(kvc, pos, w1, w2):
    n8, nseg, half = kvc.shape
    G = n8 // 2
    hid = w1.shape[2]
    return pl.pallas_call(
        _cmp_kernel,
        grid=(n8,),
        in_specs=[pl.BlockSpec((None, nseg, half), lambda i: (i, 0, 0)),
                  pl.BlockSpec((2, half), lambda i: (0, 0)),
                  pl.BlockSpec((None, 2 * half, hid), lambda i: (i // G, 0, 0)),
                  pl.BlockSpec((None, hid, HEAD_DIM), lambda i: (i // G, 0, 0))],
        out_specs=pl.BlockSpec((None, nseg, HEAD_DIM), lambda i: (i, 0, 0)),
        out_shape=jax.ShapeDtypeStruct((n8, nseg, HEAD_DIM), BF16),
        compiler_params=_params(("arbitrary",)),
        name="nsa_compress",
    )(kvc, pos, w1, w2)


def _lane_reduce(x, op, reduce):
    n = x.shape[-1]
    if n % LANES == 0 and n > LANES:
        parts = [x[..., i * LANES:(i + 1) * LANES] for i in range(n // LANES)]
        while len(parts) > 1:
            parts = [op(parts[i], parts[i + 1]) if i + 1 < len(parts) else parts[i] for i in range(0, len(parts), 2)]
        x = parts[0]
    return reduce(x, axis=-1, keepdims=True)


def _row_max(x):
    return _lane_reduce(x, jnp.maximum, jnp.max)


def _row_sum(x):
    return _lane_reduce(x, jnp.add, jnp.sum)


def _softmax_rows(s, valid):
    m = _row_max(jnp.where(valid, s, NEG_INF))
    e = jnp.where(valid, jnp.exp(jnp.where(valid, s - m, 0.0)), 0.0)
    den = _row_sum(e)
    return e / jnp.where(den > 0.0, den, 1.0)


def _nsa_kernel(q_ref, gate_ref, slope_ref, saug_ref, kc_ref, vc_ref, ks_ref, vs_ref, kw_ref, vw_ref,
                wselt_ref, kaug_ref, o_ref, m_ref, l_ref, acc_ref, *, n_sel, tk):
    qb = pl.program_id(1)
    start = qb * Q_BLOCK
    HG = NSA_HPG
    R = HG * Q_BLOCK
    D = HEAD_DIM
    q = jnp.concatenate([q_ref[:, hh * D:(hh + 1) * D] for hh in range(HG)], axis=0)
    slope = slope_ref[...]
    tq = start + lax.broadcasted_iota(jnp.int32, (Q_BLOCK, 1), 0)
    tq_f = tq.astype(F32)

    ncb = kc_ref.shape[0]
    cidx = lax.broadcasted_iota(jnp.int32, (1, ncb), 1)
    centre = cidx.astype(F32) * CMP_STRIDE + (CMP_BLOCK - 1) / 2
    valid_c = ((cidx * CMP_STRIDE + (CMP_BLOCK - 1)) <= tq)[None]
    s_c = _dot_nt(q, kc_ref[...]).reshape(HG, Q_BLOCK, ncb) - slope * (tq_f - centre)[None]
    p_c = _softmax_rows(s_c, valid_c)
    o_cmp = _dot(p_c.reshape(R, ncb).astype(BF16), vc_ref[...])
    imp = jnp.sum(p_c, axis=0)

    nsb = wselt_ref.shape[0]
    wselt = wselt_ref[...]
    i1 = imp.astype(BF16)
    r1 = imp - i1.astype(F32)
    i2 = r1.astype(BF16)
    i3 = (r1 - i2.astype(F32)).astype(BF16)
    imp_t = _dot_nt(wselt, i1) + (_dot_nt(wselt, i2) + _dot_nt(wselt, i3))
    tq_row = start + lax.broadcasted_iota(jnp.int32, (1, Q_BLOCK), 1)
    blk = lax.broadcasted_iota(jnp.int32, (nsb, Q_BLOCK), 0)
    blk_f = blk.astype(F32)
    cur = tq_row // SEL_BLOCK
    forced = (blk == 0) | (blk == cur) | (blk == cur - 1)
    visible = blk * SEL_BLOCK <= tq_row
    score = jnp.where(visible, jnp.where(forced, FORCE_SCORE, imp_t), NEG_INF)
    sel_t = jnp.zeros((nsb, Q_BLOCK), F32)
    for _ in range(n_sel):
        mx = jnp.max(score, axis=0, keepdims=True)
        first = jnp.min(jnp.where(score == mx, blk_f, float(nsb)), axis=0, keepdims=True)
        pick = blk_f == first
        sel_t = jnp.where(pick, 1.0, sel_t)
        score = jnp.where(pick, -jnp.inf, score)
    neg = (sel_t.T - 1.0) * (-NEG_INF)
    bpt = tk // SEL_BLOCK
    n_key_tiles = nsb // bpt
    tile_any = jnp.max(jnp.max(sel_t.reshape(n_key_tiles, bpt, Q_BLOCK), axis=2, keepdims=True), axis=1, keepdims=True)
    tile_bit = lax.shift_left(jnp.int32(1), lax.broadcasted_iota(jnp.int32, (n_key_tiles, 1, 1), 0))
    tile_bits = jnp.sum(jnp.where(tile_any > 0.0, tile_bit, 0))

    n_tiles = (start + Q_BLOCK - 1) // tk + 1
    kaug = kaug_ref[...]
    saug = saug_ref[...]
    lane_q = lax.broadcasted_iota(jnp.int32, (Q_BLOCK, LANES), 1)
    m_ref[...] = jnp.full(m_ref.shape, NEG_INF, F32)
    l_ref[...] = jnp.zeros(l_ref.shape, F32)
    acc_ref[...] = jnp.zeros(acc_ref.shape, F32)

    def sel_tile(kt, causal):
        k0 = pl.multiple_of(kt * tk, tk)
        shifted = pltpu.roll(neg, (nsb - kt * bpt) % nsb, 1)[:, :LANES]
        qa = jnp.concatenate([jnp.where(lane_q < bpt, shifted, saug[hh]).astype(BF16) for hh in range(HG)], axis=0)
        lhs = jnp.concatenate([q, qa], axis=1)
        rhs = jnp.concatenate([ks_ref[pl.ds(k0, tk), :], kaug], axis=1)
        v_tile = vs_ref[pl.ds(k0, tk), :]
        rowoff = slope * (tq - k0).astype(F32)[None]
        nh = HG // SEL_HEAD_SPLIT
        rh = nh * Q_BLOCK
        scores = [_dot_nt(lhs[g2 * rh:(g2 + 1) * rh], rhs).reshape(nh, Q_BLOCK, tk) for g2 in range(SEL_HEAD_SPLIT)]
        for g2 in range(SEL_HEAD_SPLIT):
            hs = slice(g2 * nh, (g2 + 1) * nh)
            s = scores[g2]
            if causal:
                kpos = k0 + lax.broadcasted_iota(jnp.int32, (1, tk), 1)
                s = jnp.where((kpos <= tq)[None], s, NEG_INF)
            off = rowoff[hs]
            m_run = m_ref[hs]
            m_new = jnp.maximum(m_run, _row_max(s) - off)
            p = jnp.exp(s - (m_new + off))
            alpha = jnp.exp(m_run - m_new)
            l_ref[hs] = alpha * l_ref[hs] + _row_sum(p)
            pv = _dot(p.reshape(rh, tk).astype(BF16), v_tile)
            acc_ref[hs] = alpha * acc_ref[hs] + pv.reshape(nh, Q_BLOCK, D)
            m_ref[hs] = m_new

    def sel_body(kt, carry):
        @pl.when((lax.shift_right_logical(tile_bits, kt) & 1) == 1)
        def _():
            sel_tile(kt, False)

        return carry

    lax.fori_loop(0, n_tiles - 1, sel_body, 0)
    sel_tile(n_tiles - 1, True)
    o_slc = acc_ref[...] / l_ref[...]

    wk = WINDOW + Q_BLOCK
    w0 = pl.multiple_of(jnp.maximum(start - WINDOW, 0), Q_BLOCK)
    kpos_w = w0 + lax.broadcasted_iota(jnp.int32, (1, wk), 1)
    dist_w = tq - kpos_w
    valid_w = ((dist_w >= 0) & (dist_w < WINDOW))[None]
    s_w = _dot_nt(q, kw_ref[pl.ds(w0, wk), :]).reshape(HG, Q_BLOCK, wk) - slope * dist_w.astype(F32)[None]
    p_w = _softmax_rows(s_w, valid_w)
    o_win = _dot(p_w.reshape(R, wk).astype(BF16), vw_ref[pl.ds(w0, wk), :])

    gates = _sigmoid(gate_ref[...])
    for hh in range(HG):
        rows = slice(hh * Q_BLOCK, (hh + 1) * Q_BLOCK)
        out = (gates[:, hh:hh + 1] * o_cmp[rows]
               + gates[:, HG + hh:HG + hh + 1] * o_slc[hh]
               + gates[:, 2 * HG + hh:2 * HG + hh + 1] * o_win[rows])
        o_ref[:, hh * D:(hh + 1) * D] = out.astype(o_ref.dtype)


def nsa_attention(q, gate_logits, slopes, k_cmp, v_cmp, k_slc, v_slc, k_win, v_win, *, tk=512):
    T = q.shape[0]
    G, HG, D = NSA_GROUPS, NSA_HPG, HEAD_DIM
    nqb = T // Q_BLOCK
    ncb = k_cmp.shape[1]
    nsb = T // SEL_BLOCK
    n_sel = min(SEL_COUNT, nsb)
    bpt = tk // SEL_BLOCK
    assert T % tk == 0 and tk % SEL_BLOCK == 0 and T >= WINDOW + Q_BLOCK and bpt == SUBLANES
    nsb_pad = -(-nsb // LANES) * LANES
    assert nsb_pad // bpt <= 32
    jj = jnp.arange(nsb_pad)[:, None]
    cc = jnp.arange(ncb)[None, :]
    off = cc - (SEL_RATIO * jj - 1)
    wselt = jnp.zeros((nsb_pad, ncb), F32)
    for o, wt in enumerate(SEL_OVERLAP_W):
        wselt = jnp.where((off == o) & (jj < nsb), wt, wselt)
    wselt = wselt.astype(BF16)
    c = jnp.arange(tk)[:, None]
    lane = jnp.arange(LANES)[None, :]
    kaug = jnp.where(lane < bpt, (c // SEL_BLOCK == lane).astype(F32),
                     jnp.where(lane < bpt + 3, (c // SEL_BLOCK * SEL_BLOCK).astype(F32),
                               jnp.where(lane < bpt + 6, (c % SEL_BLOCK).astype(F32), 0.0))).astype(BF16)
    s1 = slopes.astype(BF16).astype(F32)
    s2 = (slopes - s1).astype(BF16).astype(F32)
    s3 = (slopes - s1 - s2).astype(BF16).astype(F32)
    pieces = jnp.stack([s1, s2, s3, s1, s2, s3], axis=-1)
    saug = jnp.pad(pieces, ((0, 0), (bpt, LANES - bpt - 6))).reshape(G, HG, 1, LANES)
    full = lambda rows: pl.BlockSpec((None, rows, D), lambda g, b: (g, 0, 0))
    return pl.pallas_call(
        functools.partial(_nsa_kernel, n_sel=n_sel, tk=tk),
        grid=(G, nqb),
        in_specs=[pl.BlockSpec((Q_BLOCK, HG * D), lambda g, b: (b, g)),
                  pl.BlockSpec((None, Q_BLOCK, 3 * HG), lambda g, b: (g, b, 0)),
                  pl.BlockSpec((None, HG, 1, 1), lambda g, b: (g, 0, 0, 0)),
                  pl.BlockSpec((None, HG, 1, LANES), lambda g, b: (g, 0, 0, 0)),
                  full(ncb), full(ncb), full(T), full(T), full(T), full(T),
                  pl.BlockSpec((nsb_pad, ncb), lambda g, b: (0, 0)),
                  pl.BlockSpec((tk, LANES), lambda g, b: (0, 0))],
        out_specs=pl.BlockSpec((Q_BLOCK, HG * D), lambda g, b: (b, g)),
        out_shape=jax.ShapeDtypeStruct((T, NSA_HEADS * D), BF16),
        scratch_shapes=[pltpu.VMEM((HG, Q_BLOCK, 1), F32),
                        pltpu.VMEM((HG, Q_BLOCK, 1), F32),
                        pltpu.VMEM((HG, Q_BLOCK, D), F32)],
        compiler_params=_params(("arbitrary", "arbitrary")),
        name="nsa_attention",
    )(q, gate_logits, slopes.reshape(G, HG, 1, 1), saug, k_cmp, v_cmp, k_slc, v_slc, k_win, v_win, wselt, kaug)


def _pad_cols(w, n):
    return jnp.pad(w, ((0, 0), (0, n - w.shape[1])))


def kernel(x, mixer_norm_g, ffn_norm_g, ffn_w_gate_up, ffn_w_down, gdn_w_in, gdn_conv_w, gdn_a_log, gdn_dt_bias, gdn_out_norm_g, gdn_w_out, kv_norm_g, kv_w, cmp_pos, cmp_w1_k, cmp_w2_k, cmp_w1_v, cmp_w2_v, nsa_w_in, nsa_w_out, final_norm_g):
    B, T, Dm = x.shape
    assert B == 1
    h = x.reshape(T, Dm)
    D = HEAD_DIM
    qk_dim = GDN_QK_HEADS * D
    v_dim = GDN_V_HEADS * D
    conv_dim = 2 * qk_dim + v_dim
    main = conv_dim + v_dim

    w_in = gdn_w_in[0]
    proj = norm_matmul(h, mixer_norm_g[0], w_in[:, :main].astype(BF16), out_dtype=F32)
    ba = norm_matmul(h, mixer_norm_g[0], _pad_cols(w_in[:, main:], LANES).astype(BF16), out_dtype=F32)
    beta_t, gc_t = gdn_gates(ba[:, :2 * GDN_V_HEADS].T, gdn_a_log[0], gdn_dt_bias[0])
    qkv = gdn_conv(proj, gdn_conv_w[0], n_qk_cols=2 * qk_dim)
    o = gdn_delta_rule(qkv, proj, gc_t.T, beta_t.T, gc_t, gdn_out_norm_g[0])
    h = matmul_residual(o, gdn_w_out[0].astype(BF16), h)
    act = norm_swiglu(h, ffn_norm_g[0], ffn_w_gate_up[0].astype(BF16))
    h = matmul_residual(act, ffn_w_down[0].astype(BF16), h)

    G = NSA_GROUPS
    n_cmp_cols = 2 * G * D
    kv_w16 = kv_w.astype(BF16)
    kvc = norm_matmul(h, kv_norm_g, kv_w16[:, :n_cmp_cols], out_dtype=F32, split=True)
    kvr = norm_matmul(h, kv_norm_g, kv_w16[:, n_cmp_cols:], out_dtype=BF16, split=True)
    nseg = T // CMP_STRIDE
    pos2 = cmp_pos.reshape(2, CMP_STRIDE * D)
    w1 = jnp.stack([cmp_w1_k, cmp_w1_v]).astype(BF16)
    w2 = jnp.stack([cmp_w2_k, cmp_w2_v]).astype(BF16)
    kv_cmp = compress_kv(kvc.reshape(2 * G, nseg, CMP_STRIDE * D), pos2, w1, w2)

    w_nsa = nsa_w_in[0]
    q_dim = NSA_HEADS * D
    q = norm_matmul(h, mixer_norm_g[1], w_nsa[:, :q_dim].astype(BF16), out_dtype=BF16, scale=D ** -0.5)
    gl = norm_matmul(h, mixer_norm_g[1], _pad_cols(w_nsa[:, q_dim:], LANES).astype(BF16), out_dtype=F32)
    gl = gl[:, :3 * NSA_HEADS].reshape(T, 3, G, NSA_HPG).transpose(2, 0, 1, 3).reshape(G, T, 3 * NSA_HPG)
    slopes = 2.0 ** (-8.0 * jnp.arange(1, NSA_HEADS + 1, dtype=F32) / NSA_HEADS)
    o = nsa_attention(q, gl, slopes, kv_cmp[:G], kv_cmp[G:], kvr[:G], kvr[G:2 * G], kvr[2 * G:3 * G], kvr[3 * G:])
    h = matmul_residual(o, nsa_w_out[0].astype(BF16), h)
    act = norm_swiglu(h, ffn_norm_g[1], ffn_w_gate_up[1].astype(BF16))
    h = matmul_residual(act, ffn_w_down[1].astype(BF16), h)
    return rmsnorm(h, final_norm_g).reshape(B, T, Dm)
```

```python
import functools

import jax
import jax.numpy as jnp
from jax import lax
from jax.experimental import pallas as pl
from jax.experimental.pallas import tpu as pltpu

F32 = jnp.float32
BF16 = jnp.bfloat16

RMS_EPS = 1e-6
L2_EPS = 1e-6
NEG_INF = -1e30
FORCE_SCORE = 1e4

GDN_QK_HEADS = 16
GDN_V_HEADS = 32
HEAD_DIM = 128
GDN_CONV = 4
GDN_CHUNK = 64

NSA_HEADS = 16
NSA_GROUPS = 4
NSA_HPG = NSA_HEADS // NSA_GROUPS
CMP_BLOCK = 32
CMP_STRIDE = 16
SEL_BLOCK = 64
SEL_COUNT = 16
WINDOW = 512
Q_BLOCK = 128
SEL_RATIO = SEL_BLOCK // CMP_STRIDE
SEL_OVERLAP_W = (1.0, 2.0, 2.0, 2.0, 1.0)

V7X_VMEM_LIMIT_BYTES = 56 * 1024 * 1024
LANES = 128
SUBLANES = 8


def _params(sem):
    return pltpu.CompilerParams(dimension_semantics=sem, vmem_limit_bytes=V7X_VMEM_LIMIT_BYTES)


def _sigmoid(x):
    return 1.0 / (1.0 + jnp.exp(-x))


def _dot(a, b):
    return jnp.dot(a, b, preferred_element_type=F32)


def _dot_nt(a, b):
    return lax.dot_general(a, b, (((1,), (1,)), ((), ())), preferred_element_type=F32)


def _split_bf16(a):
    hi = a.astype(BF16)
    lo = (a - hi.astype(F32)).astype(BF16)
    return hi, lo


def _dot_hi(a, b):
    ah, al = _split_bf16(a)
    bh, bl = _split_bf16(b)
    return _dot(ah, bh) + (_dot(ah, bl) + _dot(al, bh))


def _norm_rows(x_ref, g_ref, xn_ref):
    x = x_ref[...]
    ms = jnp.mean(x * x, axis=-1, keepdims=True)
    xn_ref[...] = (x * lax.rsqrt(ms + RMS_EPS) * g_ref[...]).astype(xn_ref.dtype)


def _norm_mm_kernel(x_ref, g_ref, w_ref, o_ref, xn_ref, *, scale, split):
    @pl.when(pl.program_id(1) == 0)
    def _():
        _norm_rows(x_ref, g_ref, xn_ref)

    acc = _dot(xn_ref[...], w_ref[...])
    if scale is not None:
        acc = acc * scale
    if split:
        for c in range(o_ref.shape[0]):
            o_ref[c] = acc[:, c * LANES:(c + 1) * LANES].astype(o_ref.dtype)
    else:
        o_ref[...] = acc.astype(o_ref.dtype)


def norm_matmul(x, g, w, *, out_dtype, scale=None, split=False, tm=1024, tn=512):
    T, D = x.shape
    N = w.shape[1]
    tn = min(tn, N)
    assert T % tm == 0 and N % tn == 0 and tn % LANES == 0
    if split:
        out_shape = jax.ShapeDtypeStruct((N // LANES, T, LANES), out_dtype)
        out_spec = pl.BlockSpec((tn // LANES, tm, LANES), lambda i, j: (j, i, 0))
    else:
        out_shape = jax.ShapeDtypeStruct((T, N), out_dtype)
        out_spec = pl.BlockSpec((tm, tn), lambda i, j: (i, j))
    return pl.pallas_call(
        functools.partial(_norm_mm_kernel, scale=scale, split=split),
        grid=(T // tm, N // tn),
        in_specs=[pl.BlockSpec((tm, D), lambda i, j: (i, 0)),
                  pl.BlockSpec((1, D), lambda i, j: (0, 0)),
                  pl.BlockSpec((D, tn), lambda i, j: (0, j))],
        out_specs=out_spec,
        out_shape=out_shape,
        scratch_shapes=[pltpu.VMEM((tm, D), BF16)],
        compiler_params=_params(("arbitrary", "arbitrary")),
        name="norm_matmul",
    )(x, g.reshape(1, D), w)


def _norm_swiglu_kernel(x_ref, g_ref, wg_ref, wu_ref, o_ref, xn_ref):
    @pl.when(pl.program_id(1) == 0)
    def _():
        _norm_rows(x_ref, g_ref, xn_ref)

    xn = xn_ref[...]
    gate = _dot(xn, wg_ref[...])
    up = _dot(xn, wu_ref[...])
    o_ref[...] = (gate * _sigmoid(gate) * up).astype(o_ref.dtype)


def norm_swiglu(x, g, w_gate_up, *, tm=1024, tn=512):
    T, D = x.shape
    F = w_gate_up.shape[1] // 2
    assert T % tm == 0 and F % tn == 0
    nj = F // tn
    return pl.pallas_call(
        _norm_swiglu_kernel,
        grid=(T // tm, nj),
        in_specs=[pl.BlockSpec((tm, D), lambda i, j: (i, 0)),
                  pl.BlockSpec((1, D), lambda i, j: (0, 0)),
                  pl.BlockSpec((D, tn), lambda i, j: (0, j)),
                  pl.BlockSpec((D, tn), lambda i, j: (0, j + nj))],
        out_specs=pl.BlockSpec((tm, tn), lambda i, j: (i, j)),
        out_shape=jax.ShapeDtypeStruct((T, F), BF16),
        scratch_shapes=[pltpu.VMEM((tm, D), BF16)],
        compiler_params=_params(("arbitrary", "arbitrary")),
        name="norm_swiglu",
    )(x, g.reshape(1, D), w_gate_up, w_gate_up)


def _mm_res_kernel(a_ref, w_ref, r_ref, o_ref):
    o_ref[...] = r_ref[...] + _dot(a_ref[...], w_ref[...])


def matmul_residual(a, w, res, *, tm=1024, tn=512):
    T, K = a.shape
    N = w.shape[1]
    assert T % tm == 0 and N % tn == 0
    return pl.pallas_call(
        _mm_res_kernel,
        grid=(T // tm, N // tn),
        in_specs=[pl.BlockSpec((tm, K), lambda i, j: (i, 0)),
                  pl.BlockSpec((K, tn), lambda i, j: (0, j)),
                  pl.BlockSpec((tm, tn), lambda i, j: (i, j))],
        out_specs=pl.BlockSpec((tm, tn), lambda i, j: (i, j)),
        out_shape=jax.ShapeDtypeStruct((T, N), F32),
        compiler_params=_params(("arbitrary", "arbitrary")),
        name="matmul_residual",
    )(a, w, res)


def _rmsnorm_kernel(x_ref, g_ref, o_ref):
    _norm_rows(x_ref, g_ref, o_ref)


def rmsnorm(x, g, *, tm=512):
    T, D = x.shape
    return pl.pallas_call(
        _rmsnorm_kernel,
        grid=(T // tm,),
        in_specs=[pl.BlockSpec((tm, D), lambda i: (i, 0)),
                  pl.BlockSpec((1, D), lambda i: (0, 0))],
        out_specs=pl.BlockSpec((tm, D), lambda i: (i, 0)),
        out_shape=jax.ShapeDtypeStruct((T, D), x.dtype),
        compiler_params=_params(("arbitrary",)),
        name="final_rmsnorm",
    )(x, g.reshape(1, D))


def _inproj_conv_kernel(x_ref, g_ref, w_ref, cw_ref, o_ref, xn_ref, carry_ref, *, n_conv, n_qk):
    i = pl.program_id(0)
    j = pl.program_id(1)

    @pl.when(j == 0)
    def _():
        _norm_rows(x_ref, g_ref, xn_ref)

    @pl.when((i == 0) & (j == 0))
    def _():
        carry_ref[...] = jnp.zeros_like(carry_ref)

    acc = _dot(xn_ref[...], w_ref[...])
    tm, tn = acc.shape

    @pl.when(j >= n_conv)
    def _():
        o_ref[...] = acc.astype(o_ref.dtype)

    @pl.when(j < n_conv)
    def _():
        halo = carry_ref[j]
        carry_ref[j] = acc[tm - SUBLANES:, :]
        w = cw_ref[...]
        top = acc[:SUBLANES]
        row8 = lax.broadcasted_iota(jnp.int32, top.shape, 0)
        y = w[GDN_CONV - 1:GDN_CONV] * acc
        y_top = w[GDN_CONV - 1:GDN_CONV] * top
        for s in range(1, GDN_CONV):
            wk = w[GDN_CONV - 1 - s:GDN_CONV - s]
            xs = pltpu.roll(acc, s, 0)
            y = y + wk * xs
            hs = pltpu.roll(halo, s, 0)
            y_top = y_top + wk * jnp.where(row8 < s, hs, xs[:SUBLANES])
        is_qk = j < n_qk

        def finish(v):
            v = v * _sigmoid(v)
            outs = []
            for hh in range(tn // HEAD_DIM):
                vh = v[:, hh * HEAD_DIM:(hh + 1) * HEAD_DIM]
                inv = lax.rsqrt(jnp.sum(vh * vh, axis=-1, keepdims=True) + L2_EPS)
                outs.append(vh * jnp.where(is_qk, inv, 1.0))
            return jnp.concatenate(outs, axis=-1)

        o_ref[...] = finish(y).astype(o_ref.dtype)
        o_ref[:SUBLANES, :] = finish(y_top).astype(o_ref.dtype)


def gdn_inproj_conv(x, g, w, conv_w, *, n_qk_cols, tm=1024, tn=512):
    T, D = x.shape
    N = w.shape[1]
    C = conv_w.shape[1]
    assert T % tm == 0 and N % tn == 0 and C % tn == 0 and n_qk_cols % tn == 0 and tn % HEAD_DIM == 0
    n_conv = C // tn
    return pl.pallas_call(
        functools.partial(_inproj_conv_kernel, n_conv=n_conv, n_qk=n_qk_cols // tn),
        grid=(T // tm, N // tn),
        in_specs=[pl.BlockSpec((tm, D), lambda i, j: (i, 0)),
                  pl.BlockSpec((1, D), lambda i, j: (0, 0)),
                  pl.BlockSpec((D, tn), lambda i, j: (0, j)),
                  pl.BlockSpec((GDN_CONV, tn), lambda i, j: (0, jnp.minimum(j, n_conv - 1)))],
        out_specs=pl.BlockSpec((tm, tn), lambda i, j: (i, j)),
        out_shape=jax.ShapeDtypeStruct((T, N), BF16),
        scratch_shapes=[pltpu.VMEM((tm, D), BF16),
                        pltpu.VMEM((n_conv, SUBLANES, tn), F32)],
        compiler_params=_params(("arbitrary", "arbitrary")),
        name="gdn_inproj_conv",
    )(x, g.reshape(1, D), w, conv_w)


def _gate_kernel(ba_ref, alog_ref, dtb_ref, beta_ref, gc_ref):
    nh = beta_ref.shape[0]
    b_raw = ba_ref[:nh, :]
    a_raw = ba_ref[nh:2 * nh, :]
    beta_ref[...] = _sigmoid(b_raw)
    xs = a_raw + dtb_ref[...]
    softplus = jnp.maximum(xs, 0.0) + jnp.log(1.0 + jnp.exp(-jnp.abs(xs)))
    g = -jnp.exp(alog_ref[...]) * softplus
    pos = lax.broadcasted_iota(jnp.int32, g.shape, 1) % GDN_CHUNK
    s = 1
    while s < GDN_CHUNK:
        g = g + jnp.where(pos >= s, pltpu.roll(g, s, 1), 0.0)
        s *= 2
    gc_ref[...] = g


def gdn_gates(ba_t, a_log, dt_bias, *, tl=2048):
    nh2, T = ba_t.shape
    nh = nh2 // 2
    tl = min(tl, T)
    assert T % tl == 0 and tl % GDN_CHUNK == 0
    return pl.pallas_call(
        _gate_kernel,
        grid=(T // tl,),
        in_specs=[pl.BlockSpec((nh2, tl), lambda i: (0, i)),
                  pl.BlockSpec((nh, 1), lambda i: (0, 0)),
                  pl.BlockSpec((nh, 1), lambda i: (0, 0))],
        out_specs=[pl.BlockSpec((nh, tl), lambda i: (0, i)),
                   pl.BlockSpec((nh, tl), lambda i: (0, i))],
        out_shape=[jax.ShapeDtypeStruct((nh, T), F32), jax.ShapeDtypeStruct((nh, T), F32)],
        compiler_params=_params(("arbitrary",)),
        name="gdn_gates",
    )(ba_t, a_log.reshape(nh, 1), dt_bias.reshape(nh, 1))


def _bdot(a, b):
    return lax.dot_general(a, b, (((2,), (1,)), ((0,), (0,))), preferred_element_type=F32)


def _bdot_nt(a, b):
    return lax.dot_general(a, b, (((2,), (2,)), ((0,), (0,))), preferred_element_type=F32)


def _gdn_kernel(q_ref, k_ref, v_ref, z_ref, gcol_ref, bcol_ref, grow_ref, ong_ref, o_ref,
                s_ref, qw_ref, bn_ref, on_ref, egl_ref):
    p = pl.program_id(0)
    c = pl.program_id(1)
    C = GDN_CHUNK
    D = HEAD_DIM
    nc = bn_ref.shape[1]
    nh = s_ref.shape[0]
    npair = q_ref.shape[1] // D
    rep = nh // npair

    @pl.when(c == 0)
    def _():
        s_ref[...] = jnp.zeros_like(s_ref)

    ri = lax.broadcasted_iota(jnp.int32, (C, C), 0)
    ci = lax.broadcasted_iota(jnp.int32, (C, C), 1)
    causal = ri >= ci
    strict = (ri > ci)[None]
    eye = jnp.where(ri == ci, 1.0, 0.0)[None]
    lane = lax.broadcasted_iota(jnp.int32, gcol_ref.shape, 1)

    qscale = D ** -0.5
    for hp in range(nh):
        qp = hp // rep
        if hp % rep == 0:
            k16 = k_ref[:, qp * D:(qp + 1) * D].reshape(nc, C, D)
            q16 = q_ref[:, qp * D:(qp + 1) * D].reshape(nc, C, D)
            k3 = k16.astype(F32)
            q3 = q16.astype(F32) * qscale
            kk = _bdot_nt(k16, k16)
            qk = _bdot_nt(q16, k16) * qscale
        h = p * nh + hp
        gc3 = jnp.sum(jnp.where(lane == h, gcol_ref[...], 0.0), axis=1, keepdims=True).reshape(nc, C, 1)
        bt3 = jnp.sum(jnp.where(lane == h, bcol_ref[...], 0.0), axis=1, keepdims=True).reshape(nc, C, 1)
        decay = jnp.stack([
            jnp.where(causal, jnp.exp(jnp.where(causal, gc3[n] - grow_ref[hp, n:n + 1, :], 0.0)), 0.0)
            for n in range(nc)])
        lmat = jnp.where(strict, kk * bt3 * decay, 0.0)
        inv = eye - lmat
        m = lmat
        for _ in range(5):
            m16 = m.astype(BF16)
            m = _bdot(m16, m16)
            inv = inv + _bdot(inv.astype(BF16), m.astype(BF16))
        inv16 = inv.astype(BF16)
        eg3 = jnp.exp(gc3)
        v3 = v_ref[:, hp * D:(hp + 1) * D].reshape(nc, C, D).astype(F32)
        u16 = _bdot(inv16, (v3 * bt3).astype(BF16)).astype(BF16)
        w16 = _bdot(inv16, (k3 * (bt3 * eg3)).astype(BF16)).astype(BF16)
        attn16 = jnp.where(causal[None], qk * decay, 0.0).astype(BF16)
        gl3 = gc3[:, C - 1:C, :]
        kd3 = k3 * jnp.exp(gl3 - gc3)
        kdt16 = jnp.stack([kd3[n].T for n in range(nc)]).astype(BF16)
        qw_ref[hp, :, C:, :] = _bdot(kdt16, w16).astype(qw_ref.dtype)
        bn_ref[hp] = _bdot(kdt16, u16)
        qw_ref[hp, :, :C, :] = (q3 * eg3 - _bdot(attn16, w16)).astype(qw_ref.dtype)
        on_ref[hp] = _bdot(attn16, u16)
        egl_ref[hp] = jnp.broadcast_to(jnp.exp(gl3), egl_ref.shape[1:])

    ong = ong_ref[...]
    states = [s_ref[hp] for hp in range(nh)]
    for n in range(nc):
        sl = slice(n * C, (n + 1) * C)
        for hp in range(nh):
            s16 = states[hp].astype(BF16)
            qw_s = _dot(qw_ref[hp, n], s16)
            o = qw_s[:C] + on_ref[hp, n]
            states[hp] = states[hp] * egl_ref[hp, n, :1, :] - qw_s[C:] + bn_ref[hp, n]
            ms = jnp.mean(o * o, axis=-1, keepdims=True)
            on = o * lax.rsqrt(ms + RMS_EPS) * ong
            z = z_ref[sl, hp * D:(hp + 1) * D].astype(F32)
            o_ref[sl, hp * D:(hp + 1) * D] = (on * (z * _sigmoid(z))).astype(o_ref.dtype)
    for hp in range(nh):
        s_ref[hp] = states[hp]


def gdn_delta_rule(proj, gc_col, beta_col, gc_row, out_norm_g, *, tb=1024, npair=4):
    T = proj.shape[0]
    H = GDN_V_HEADS
    rep = GDN_V_HEADS // GDN_QK_HEADS
    kq = GDN_QK_HEADS
    C = GDN_CHUNK
    D = HEAD_DIM
    assert T % tb == 0 and tb % C == 0 and kq % npair == 0
    nc = tb // C
    nh = npair * rep
    k_blk0 = kq // npair
    v_blk0 = 2 * kq // nh
    z_blk0 = (2 * kq + H) // nh
    return pl.pallas_call(
        _gdn_kernel,
        grid=(kq // npair, T // tb),
        in_specs=[pl.BlockSpec((tb, npair * D), lambda p, c: (c, p)),
                  pl.BlockSpec((tb, npair * D), lambda p, c: (c, k_blk0 + p)),
                  pl.BlockSpec((tb, nh * D), lambda p, c: (c, v_blk0 + p)),
                  pl.BlockSpec((tb, nh * D), lambda p, c: (c, z_blk0 + p)),
                  pl.BlockSpec((tb, H), lambda p, c: (c, 0)),
                  pl.BlockSpec((tb, H), lambda p, c: (c, 0)),
                  pl.BlockSpec((nh, nc, C), lambda p, c: (p, c, 0)),
                  pl.BlockSpec((1, D), lambda p, c: (0, 0))],
        out_specs=pl.BlockSpec((tb, nh * D), lambda p, c: (c, p)),
        out_shape=jax.ShapeDtypeStruct((T, H * D), BF16),
        scratch_shapes=[pltpu.VMEM((nh, D, D), F32),
                        pltpu.VMEM((nh, nc, C + D, D), BF16),
                        pltpu.VMEM((nh, nc, D, D), F32),
                        pltpu.VMEM((nh, nc, C, D), F32),
                        pltpu.VMEM((nh, nc, SUBLANES, D), F32)],
        compiler_params=_params(("arbitrary", "arbitrary")),
        name="gdn_delta_rule",
    )(proj, proj, proj, proj, gc_col, beta_col, gc_row.reshape(H, T // C, C), out_norm_g.reshape(1, D))


def _cmp_kernel(x_ref, pos_ref, w1_ref, w2_ref, o_ref):
    x = x_ref[...]
    half = x.shape[1]
    top = (x + pos_ref[0:1, :]).astype(BF16)
    bot = (x + pos_ref[1:2, :]).astype(BF16)
    a = _dot(top, w1_ref[:half, :])
    b = _dot(bot, w1_ref[half:, :])
    pre = a + pltpu.roll(b, x.shape[0] - 1, 0)
    hid = pre * _sigmoid(pre)
    o_ref[...] = _dot(hid.astype(BF16), w2_ref[...]).astype(o_ref.dtype)


def compress_kv(kvc, pos, w1, w2):
    n8, nseg, half = kvc.shape
    G = n8 // 2
    hid = w1.shape[2]
    return pl.pallas_call(
        _cmp_kernel,
        grid=(n8,),
        in_specs=[pl.BlockSpec((None, nseg, half), lambda i: (i, 0, 0)),
                  pl.BlockSpec((2, half), lambda i: (0, 0)),
                  pl.BlockSpec((None, 2 * half, hid), lambda i: (i // G, 0, 0)),
                  pl.BlockSpec((None, hid, HEAD_DIM), lambda i: (i // G, 0, 0))],
        out_specs=pl.BlockSpec((None, nseg, HEAD_DIM), lambda i: (i, 0, 0)),
        out_shape=jax.ShapeDtypeStruct((n8, nseg, HEAD_DIM), BF16),
        compiler_params=_params(("arbitrary",)),
        name="nsa_compress",
    )(kvc, pos, w1, w2)


def _lane_reduce(x, op, reduce):
    n = x.shape[-1]
    if n % LANES == 0 and n > LANES:
        parts = [x[..., i * LANES:(i + 1) * LANES] for i in range(n // LANES)]
        while len(parts) > 1:
            parts = [op(parts[i], parts[i + 1]) if i + 1 < len(parts) else parts[i] for i in range(0, len(parts), 2)]
        x = parts[0]
    return reduce(x, axis=-1, keepdims=True)


def _row_max(x):
    return _lane_reduce(x, jnp.maximum, jnp.max)


def _row_sum(x):
    return _lane_reduce(x, jnp.add, jnp.sum)


def _ones_col(n):
    return jnp.where(lax.broadcasted_iota(jnp.int32, (n, LANES), 1) == 0, 1.0, 0.0).astype(BF16)


def _nsa_kernel(q_ref, gate_ref, slope_ref, saug_ref, kc_ref, vc_ref, ks_ref, vs_ref, kw_ref, vw_ref,
                wselt_ref, kaug_ref, o_ref, m_ref, l_ref, acc_ref, *, n_sel, tk):
    qb = pl.program_id(1)
    start = qb * Q_BLOCK
    HG = NSA_HPG
    R = HG * Q_BLOCK
    D = HEAD_DIM
    q = jnp.concatenate([q_ref[:, hh * D:(hh + 1) * D] for hh in range(HG)], axis=0)
    slope = slope_ref[...]
    tq = start + lax.broadcasted_iota(jnp.int32, (Q_BLOCK, 1), 0)
    tq_f = tq.astype(F32)

    ncb = kc_ref.shape[0]
    cidx = lax.broadcasted_iota(jnp.int32, (1, ncb), 1)
    centre = cidx.astype(F32) * CMP_STRIDE + (CMP_BLOCK - 1) / 2
    valid_c = ((cidx * CMP_STRIDE + (CMP_BLOCK - 1)) <= tq)[None]
    s_c = _dot_nt(q, kc_ref[...]).reshape(HG, Q_BLOCK, ncb) - slope * (tq_f - centre)[None]
    s_c = jnp.where(valid_c, s_c, NEG_INF)
    e_c = jnp.where(valid_c, jnp.exp(s_c - _row_max(s_c)), 0.0)
    den_c = _row_sum(e_c)
    p_c = e_c * (1.0 / jnp.where(den_c > 0.0, den_c, 1.0))
    o_cmp = _dot(p_c.reshape(R, ncb).astype(BF16), vc_ref[...])
    imp = jnp.sum(p_c, axis=0)

    nsb = wselt_ref.shape[0]
    wselt = wselt_ref[...]
    i1 = imp.astype(BF16)
    r1 = imp - i1.astype(F32)
    i2 = r1.astype(BF16)
    i3 = (r1 - i2.astype(F32)).astype(BF16)
    imp_t = _dot_nt(wselt, i1) + (_dot_nt(wselt, i2) + _dot_nt(wselt, i3))
    tq_row = start + lax.broadcasted_iota(jnp.int32, (1, Q_BLOCK), 1)
    blk = lax.broadcasted_iota(jnp.int32, (nsb, Q_BLOCK), 0)
    blk_f = blk.astype(F32)
    cur = tq_row // SEL_BLOCK
    forced = (blk == 0) | (blk == cur) | (blk == cur - 1)
    visible = blk * SEL_BLOCK <= tq_row
    score = jnp.where(visible, jnp.where(forced, FORCE_SCORE, imp_t), NEG_INF)
    sel_t = jnp.zeros((nsb, Q_BLOCK), F32)
    for _ in range(n_sel):
        mx = jnp.max(score, axis=0, keepdims=True)
        first = jnp.min(jnp.where(score == mx, blk_f, float(nsb)), axis=0, keepdims=True)
        pick = blk_f == first
        sel_t = jnp.where(pick, 1.0, sel_t)
        score = jnp.where(pick, -jnp.inf, score)
    neg = (sel_t.T - 1.0) * (-NEG_INF)
    bpt = tk // SEL_BLOCK
    n_key_tiles = nsb // bpt
    tile_any = jnp.max(jnp.max(sel_t.reshape(n_key_tiles, bpt, Q_BLOCK), axis=2, keepdims=True), axis=1, keepdims=True)
    tile_bit = lax.shift_left(jnp.int32(1), lax.broadcasted_iota(jnp.int32, (n_key_tiles, 1, 1), 0))
    tile_bits = jnp.sum(jnp.where(tile_any > 0.0, tile_bit, 0))

    n_tiles = (start + Q_BLOCK - 1) // tk + 1
    kaug = kaug_ref[...]
    saug = saug_ref[...]
    lane_q = lax.broadcasted_iota(jnp.int32, (Q_BLOCK, LANES), 1)
    m_ref[...] = jnp.full(m_ref.shape, NEG_INF, F32)
    l_ref[...] = jnp.zeros(l_ref.shape, F32)
    acc_ref[...] = jnp.zeros(acc_ref.shape, F32)

    def sel_tile(kt, causal):
        k0 = pl.multiple_of(kt * tk, tk)
        shifted = pltpu.roll(neg, (nsb - kt * bpt) % nsb, 1)[:, :LANES]
        qa = jnp.concatenate([jnp.where(lane_q < bpt, shifted, saug[hh]).astype(BF16) for hh in range(HG)], axis=0)
        lhs = jnp.concatenate([q, qa], axis=1)
        rhs = jnp.concatenate([ks_ref[pl.ds(k0, tk), :], kaug], axis=1)
        v_one = jnp.concatenate([vs_ref[pl.ds(k0, tk), :], _ones_col(tk)], axis=1)
        rowoff = slope * (tq - k0).astype(F32)[None]
        s = _dot_nt(lhs, rhs).reshape(HG, Q_BLOCK, tk)
        if causal:
            kpos = k0 + lax.broadcasted_iota(jnp.int32, (1, tk), 1)
            s = jnp.where((kpos <= tq)[None], s, NEG_INF)
        m_run = m_ref[...]
        m_new = jnp.maximum(m_run, _row_max(s) - rowoff)
        p = jnp.exp(s - (m_new + rowoff))
        alpha = jnp.exp(m_run - m_new)
        pv = _dot(p.reshape(R, tk).astype(BF16), v_one).reshape(HG, Q_BLOCK, 2 * LANES)
        l_ref[...] = alpha * l_ref[...] + pv[:, :, D:D + 1]
        acc_ref[...] = alpha * acc_ref[...] + pv[:, :, :D]
        m_ref[...] = m_new

    def sel_body(kt, carry):
        @pl.when((lax.shift_right_logical(tile_bits, kt) & 1) == 1)
        def _():
            sel_tile(kt, False)

        return carry

    lax.fori_loop(0, n_tiles - 1, sel_body, 0)
    sel_tile(n_tiles - 1, True)
    o_slc = acc_ref[...] / l_ref[...]

    wk = WINDOW + Q_BLOCK
    w0 = pl.multiple_of(jnp.maximum(start - WINDOW, 0), Q_BLOCK)
    kpos_w = w0 + lax.broadcasted_iota(jnp.int32, (1, wk), 1)
    dist_w = tq - kpos_w
    valid_w = ((dist_w >= 0) & (dist_w < WINDOW))[None]
    s_w = _dot_nt(q, kw_ref[pl.ds(w0, wk), :]).reshape(HG, Q_BLOCK, wk) - slope * dist_w.astype(F32)[None]
    s_w = jnp.where(valid_w, s_w, NEG_INF)
    e_w = jnp.exp(s_w - _row_max(s_w))
    den_w = _row_sum(e_w).reshape(R, 1)
    o_win = _dot(e_w.reshape(R, wk).astype(BF16), vw_ref[pl.ds(w0, wk), :]) / den_w

    gates = _sigmoid(gate_ref[...])
    for hh in range(HG):
        rows = slice(hh * Q_BLOCK, (hh + 1) * Q_BLOCK)
        out = (gates[:, hh:hh + 1] * o_cmp[rows]
               + gates[:, HG + hh:HG + hh + 1] * o_slc[hh]
               + gates[:, 2 * HG + hh:2 * HG + hh + 1] * o_win[rows])
        o_ref[:, hh * D:(hh + 1) * D] = out.astype(o_ref.dtype)


def nsa_attention(q, gate_logits, slopes, k_cmp, v_cmp, k_slc, v_slc, k_win, v_win, *, tk=512):
    T = q.shape[0]
    G, HG, D = NSA_GROUPS, NSA_HPG, HEAD_DIM
    nqb = T // Q_BLOCK
    ncb = k_cmp.shape[1]
    nsb = T // SEL_BLOCK
    n_sel = min(SEL_COUNT, nsb)
    bpt = tk // SEL_BLOCK
    assert T % tk == 0 and tk % SEL_BLOCK == 0 and T >= WINDOW + Q_BLOCK and bpt == SUBLANES
    nsb_pad = -(-nsb // LANES) * LANES
    assert nsb_pad // bpt <= 32
    jj = jnp.arange(nsb_pad)[:, None]
    cc = jnp.arange(ncb)[None, :]
    off = cc - (SEL_RATIO * jj - 1)
    wselt = jnp.zeros((nsb_pad, ncb), F32)
    for o, wt in enumerate(SEL_OVERLAP_W):
        wselt = jnp.where((off == o) & (jj < nsb), wt, wselt)
    wselt = wselt.astype(BF16)
    c = jnp.arange(tk)[:, None]
    lane = jnp.arange(LANES)[None, :]
    kaug = jnp.where(lane < bpt, (c // SEL_BLOCK == lane).astype(F32),
                     jnp.where(lane < bpt + 3, (c // SEL_BLOCK * SEL_BLOCK).astype(F32),
                               jnp.where(lane < bpt + 6, (c % SEL_BLOCK).astype(F32), 0.0))).astype(BF16)
    s1 = slopes.astype(BF16).astype(F32)
    s2 = (slopes - s1).astype(BF16).astype(F32)
    s3 = (slopes - s1 - s2).astype(BF16).astype(F32)
    pieces = jnp.stack([s1, s2, s3, s1, s2, s3], axis=-1)
    saug = jnp.pad(pieces, ((0, 0), (bpt, LANES - bpt - 6))).reshape(G, HG, 1, LANES)
    full = lambda rows: pl.BlockSpec((None, rows, D), lambda g, b: (g, 0, 0))
    return pl.pallas_call(
        functools.partial(_nsa_kernel, n_sel=n_sel, tk=tk),
        grid=(G, nqb),
        in_specs=[pl.BlockSpec((Q_BLOCK, HG * D), lambda g, b: (b, g)),
                  pl.BlockSpec((None, Q_BLOCK, 3 * HG), lambda g, b: (g, b, 0)),
                  pl.BlockSpec((None, HG, 1, 1), lambda g, b: (g, 0, 0, 0)),
                  pl.BlockSpec((None, HG, 1, LANES), lambda g, b: (g, 0, 0, 0)),
                  full(ncb), full(ncb), full(T), full(T), full(T), full(T),
                  pl.BlockSpec((nsb_pad, ncb), lambda g, b: (0, 0)),
                  pl.BlockSpec((tk, LANES), lambda g, b: (0, 0))],
        out_specs=pl.BlockSpec((Q_BLOCK, HG * D), lambda g, b: (b, g)),
        out_shape=jax.ShapeDtypeStruct((T, NSA_HEADS * D), BF16),
        scratch_shapes=[pltpu.VMEM((HG, Q_BLOCK, 1), F32),
                        pltpu.VMEM((HG, Q_BLOCK, 1), F32),
                        pltpu.VMEM((HG, Q_BLOCK, D), F32)],
        compiler_params=_params(("arbitrary", "arbitrary")),
        name="nsa_attention",
    )(q, gate_logits, slopes.reshape(G, HG, 1, 1), saug, k_cmp, v_cmp, k_slc, v_slc, k_win, v_win, wselt, kaug)


def _pad_cols(w, n):
    return jnp.pad(w, ((0, 0), (0, n - w.shape[1])))


def kernel(x, mixer_norm_g, ffn_norm_g, ffn_w_gate_up, ffn_w_down, gdn_w_in, gdn_conv_w, gdn_a_log, gdn_dt_bias, gdn_out_norm_g, gdn_w_out, kv_norm_g, kv_w, cmp_pos, cmp_w1_k, cmp_w2_k, cmp_w1_v, cmp_w2_v, nsa_w_in, nsa_w_out, final_norm_g):
    B, T, Dm = x.shape
    assert B == 1
    h = x.reshape(T, Dm)
    D = HEAD_DIM
    qk_dim = GDN_QK_HEADS * D
    v_dim = GDN_V_HEADS * D
    conv_dim = 2 * qk_dim + v_dim
    main = conv_dim + v_dim

    w_in = gdn_w_in[0]
    proj = gdn_inproj_conv(h, mixer_norm_g[0], w_in[:, :main].astype(BF16), gdn_conv_w[0],
                           n_qk_cols=2 * qk_dim)
    ba = norm_matmul(h, mixer_norm_g[0], _pad_cols(w_in[:, main:], LANES).astype(BF16), out_dtype=F32)
    beta_t, gc_t = gdn_gates(ba[:, :2 * GDN_V_HEADS].T, gdn_a_log[0], gdn_dt_bias[0])
    o = gdn_delta_rule(proj, gc_t.T, beta_t.T, gc_t, gdn_out_norm_g[0])
    h = matmul_residual(o, gdn_w_out[0].astype(BF16), h)
    act = norm_swiglu(h, ffn_norm_g[0], ffn_w_gate_up[0].astype(BF16))
    h = matmul_residual(act, ffn_w_down[0].astype(BF16), h)

    G = NSA_GROUPS
    n_cmp_cols = 2 * G * D
    kv_w16 = kv_w.astype(BF16)
    kvc = norm_matmul(h, kv_norm_g, kv_w16[:, :n_cmp_cols], out_dtype=F32, split=True)
    kvr = norm_matmul(h, kv_norm_g, kv_w16[:, n_cmp_cols:], out_dtype=BF16, split=True)
    nseg = T // CMP_STRIDE
    pos2 = cmp_pos.reshape(2, CMP_STRIDE * D)
    w1 = jnp.stack([cmp_w1_k, cmp_w1_v]).astype(BF16)
    w2 = jnp.stack([cmp_w2_k, cmp_w2_v]).astype(BF16)
    kv_cmp = compress_kv(kvc.reshape(2 * G, nseg, CMP_STRIDE * D), pos2, w1, w2)

    w_nsa = nsa_w_in[0]
    q_dim = NSA_HEADS * D
    q = norm_matmul(h, mixer_norm_g[1], w_nsa[:, :q_dim].astype(BF16), out_dtype=BF16, scale=D ** -0.5)
    gl = norm_matmul(h, mixer_norm_g[1], _pad_cols(w_nsa[:, q_dim:], LANES).astype(BF16), out_dtype=F32)
    gl = gl[:, :3 * NSA_HEADS].reshape(T, 3, G, NSA_HPG).transpose(2, 0, 1, 3).reshape(G, T, 3 * NSA_HPG)
    slopes = 2.0 ** (-8.0 * jnp.arange(1, NSA_HEADS + 1, dtype=F32) / NSA_HEADS)
    o = nsa_attention(q, gl, slopes, kv_cmp[:G], kv_cmp[G:], kvr[:G], kvr[G:2 * G], kvr[2 * G:3 * G], kvr[3 * G:])
    h = matmul_residual(o, nsa_w_out[0].astype(BF16), h)
    act = norm_swiglu(h, ffn_norm_g[1], ffn_w_gate_up[1].astype(BF16))
    h = matmul_residual(act, ffn_w_down[1].astype(BF16), h)
    return rmsnorm(h, final_norm_g).reshape(B, T, Dm)
```

```python
import functools

import jax
import jax.numpy as jnp
from jax import lax
from jax.experimental import pallas as pl
from jax.experimental.pallas import tpu as pltpu

F32 = jnp.float32
BF16 = jnp.bfloat16

RMS_EPS = 1e-6
L2_EPS = 1e-6
NEG_INF = -1e30
FORCE_SCORE = 1e4

GDN_QK_HEADS = 16
GDN_V_HEADS = 32
HEAD_DIM = 128
GDN_CONV = 4
GDN_CHUNK = 64

NSA_HEADS = 16
NSA_GROUPS = 4
NSA_HPG = NSA_HEADS // NSA_GROUPS
CMP_BLOCK = 32
CMP_STRIDE = 16
SEL_BLOCK = 64
SEL_COUNT = 16
WINDOW = 512
Q_BLOCK = 256
SEL_RATIO = SEL_BLOCK // CMP_STRIDE
SEL_OVERLAP_W = (1.0, 2.0, 2.0, 2.0, 1.0)

V7X_VMEM_LIMIT_BYTES = 56 * 1024 * 1024
LANES = 128
SUBLANES = 8


def _params(sem):
    return pltpu.CompilerParams(dimension_semantics=sem, vmem_limit_bytes=V7X_VMEM_LIMIT_BYTES)


def _sigmoid(x):
    return 1.0 / (1.0 + jnp.exp(-x))


def _dot(a, b):
    return jnp.dot(a, b, preferred_element_type=F32)


def _dot_nt(a, b):
    return lax.dot_general(a, b, (((1,), (1,)), ((), ())), preferred_element_type=F32)


def _split_bf16(a):
    hi = a.astype(BF16)
    lo = (a - hi.astype(F32)).astype(BF16)
    return hi, lo


def _dot_hi(a, b):
    ah, al = _split_bf16(a)
    bh, bl = _split_bf16(b)
    return _dot(ah, bh) + (_dot(ah, bl) + _dot(al, bh))


def _norm_rows(x_ref, g_ref, xn_ref):
    x = x_ref[...]
    ms = jnp.mean(x * x, axis=-1, keepdims=True)
    xn_ref[...] = (x * lax.rsqrt(ms + RMS_EPS) * g_ref[...]).astype(xn_ref.dtype)


def _norm_mm_kernel(x_ref, g_ref, w_ref, o_ref, xn_ref, *, scale, split):
    @pl.when(pl.program_id(1) == 0)
    def _():
        _norm_rows(x_ref, g_ref, xn_ref)

    acc = _dot(xn_ref[...], w_ref[...])
    if scale is not None:
        acc = acc * scale
    if split:
        for c in range(o_ref.shape[0]):
            o_ref[c] = acc[:, c * LANES:(c + 1) * LANES].astype(o_ref.dtype)
    else:
        o_ref[...] = acc.astype(o_ref.dtype)


def norm_matmul(x, g, w, *, out_dtype, scale=None, split=False, tm=1024, tn=512):
    T, D = x.shape
    N = w.shape[1]
    tn = min(tn, N)
    assert T % tm == 0 and N % tn == 0 and tn % LANES == 0
    if split:
        out_shape = jax.ShapeDtypeStruct((N // LANES, T, LANES), out_dtype)
        out_spec = pl.BlockSpec((tn // LANES, tm, LANES), lambda i, j: (j, i, 0))
    else:
        out_shape = jax.ShapeDtypeStruct((T, N), out_dtype)
        out_spec = pl.BlockSpec((tm, tn), lambda i, j: (i, j))
    return pl.pallas_call(
        functools.partial(_norm_mm_kernel, scale=scale, split=split),
        grid=(T // tm, N // tn),
        in_specs=[pl.BlockSpec((tm, D), lambda i, j: (i, 0)),
                  pl.BlockSpec((1, D), lambda i, j: (0, 0)),
                  pl.BlockSpec((D, tn), lambda i, j: (0, j))],
        out_specs=out_spec,
        out_shape=out_shape,
        scratch_shapes=[pltpu.VMEM((tm, D), BF16)],
        compiler_params=_params(("arbitrary", "arbitrary")),
        name="norm_matmul",
    )(x, g.reshape(1, D), w)


def _norm_swiglu_kernel(x_ref, g_ref, wg_ref, wu_ref, o_ref, xn_ref):
    @pl.when(pl.program_id(1) == 0)
    def _():
        _norm_rows(x_ref, g_ref, xn_ref)

    xn = xn_ref[...]
    gate = _dot(xn, wg_ref[...])
    up = _dot(xn, wu_ref[...])
    o_ref[...] = (gate * _sigmoid(gate) * up).astype(o_ref.dtype)


def norm_swiglu(x, g, w_gate_up, *, tm=1024, tn=512):
    T, D = x.shape
    F = w_gate_up.shape[1] // 2
    assert T % tm == 0 and F % tn == 0
    nj = F // tn
    return pl.pallas_call(
        _norm_swiglu_kernel,
        grid=(T // tm, nj),
        in_specs=[pl.BlockSpec((tm, D), lambda i, j: (i, 0)),
                  pl.BlockSpec((1, D), lambda i, j: (0, 0)),
                  pl.BlockSpec((D, tn), lambda i, j: (0, j)),
                  pl.BlockSpec((D, tn), lambda i, j: (0, j + nj))],
        out_specs=pl.BlockSpec((tm, tn), lambda i, j: (i, j)),
        out_shape=jax.ShapeDtypeStruct((T, F), BF16),
        scratch_shapes=[pltpu.VMEM((tm, D), BF16)],
        compiler_params=_params(("arbitrary", "arbitrary")),
        name="norm_swiglu",
    )(x, g.reshape(1, D), w_gate_up, w_gate_up)


def _mm_res_kernel(a_ref, w_ref, r_ref, o_ref):
    o_ref[...] = r_ref[...] + _dot(a_ref[...], w_ref[...])


def matmul_residual(a, w, res, *, tm=1024, tn=512):
    T, K = a.shape
    N = w.shape[1]
    assert T % tm == 0 and N % tn == 0
    return pl.pallas_call(
        _mm_res_kernel,
        grid=(T // tm, N // tn),
        in_specs=[pl.BlockSpec((tm, K), lambda i, j: (i, 0)),
                  pl.BlockSpec((K, tn), lambda i, j: (0, j)),
                  pl.BlockSpec((tm, tn), lambda i, j: (i, j))],
        out_specs=pl.BlockSpec((tm, tn), lambda i, j: (i, j)),
        out_shape=jax.ShapeDtypeStruct((T, N), F32),
        compiler_params=_params(("arbitrary", "arbitrary")),
        name="matmul_residual",
    )(a, w, res)


def _rmsnorm_kernel(x_ref, g_ref, o_ref):
    _norm_rows(x_ref, g_ref, o_ref)


def rmsnorm(x, g, *, tm=512):
    T, D = x.shape
    return pl.pallas_call(
        _rmsnorm_kernel,
        grid=(T // tm,),
        in_specs=[pl.BlockSpec((tm, D), lambda i: (i, 0)),
                  pl.BlockSpec((1, D), lambda i: (0, 0))],
        out_specs=pl.BlockSpec((tm, D), lambda i: (i, 0)),
        out_shape=jax.ShapeDtypeStruct((T, D), x.dtype),
        compiler_params=_params(("arbitrary",)),
        name="final_rmsnorm",
    )(x, g.reshape(1, D))


def _inproj_conv_kernel(x_ref, g_ref, w_ref, cw_ref, o_ref, xn_ref, carry_ref, *, n_conv, n_qk):
    i = pl.program_id(0)
    j = pl.program_id(1)

    @pl.when(j == 0)
    def _():
        _norm_rows(x_ref, g_ref, xn_ref)

    @pl.when((i == 0) & (j == 0))
    def _():
        carry_ref[...] = jnp.zeros_like(carry_ref)

    acc = _dot(xn_ref[...], w_ref[...])
    tm, tn = acc.shape

    @pl.when(j >= n_conv)
    def _():
        o_ref[...] = acc.astype(o_ref.dtype)

    @pl.when(j < n_conv)
    def _():
        halo = carry_ref[j]
        carry_ref[j] = acc[tm - SUBLANES:, :]
        w = cw_ref[...]
        top = acc[:SUBLANES]
        row8 = lax.broadcasted_iota(jnp.int32, top.shape, 0)
        y = w[GDN_CONV - 1:GDN_CONV] * acc
        y_top = w[GDN_CONV - 1:GDN_CONV] * top
        for s in range(1, GDN_CONV):
            wk = w[GDN_CONV - 1 - s:GDN_CONV - s]
            xs = pltpu.roll(acc, s, 0)
            y = y + wk * xs
            hs = pltpu.roll(halo, s, 0)
            y_top = y_top + wk * jnp.where(row8 < s, hs, xs[:SUBLANES])
        is_qk = j < n_qk

        def finish(v):
            v = v * _sigmoid(v)
            outs = []
            for hh in range(tn // HEAD_DIM):
                vh = v[:, hh * HEAD_DIM:(hh + 1) * HEAD_DIM]
                inv = lax.rsqrt(jnp.sum(vh * vh, axis=-1, keepdims=True) + L2_EPS)
                outs.append(vh * jnp.where(is_qk, inv, 1.0))
            return jnp.concatenate(outs, axis=-1)

        o_ref[...] = finish(y).astype(o_ref.dtype)
        o_ref[:SUBLANES, :] = finish(y_top).astype(o_ref.dtype)


def gdn_inproj_conv(x, g, w, conv_w, *, n_qk_cols, tm=1024, tn=512):
    T, D = x.shape
    N = w.shape[1]
    C = conv_w.shape[1]
    assert T % tm == 0 and N % tn == 0 and C % tn == 0 and n_qk_cols % tn == 0 and tn % HEAD_DIM == 0
    n_conv = C // tn
    return pl.pallas_call(
        functools.partial(_inproj_conv_kernel, n_conv=n_conv, n_qk=n_qk_cols // tn),
        grid=(T // tm, N // tn),
        in_specs=[pl.BlockSpec((tm, D), lambda i, j: (i, 0)),
                  pl.BlockSpec((1, D), lambda i, j: (0, 0)),
                  pl.BlockSpec((D, tn), lambda i, j: (0, j)),
                  pl.BlockSpec((GDN_CONV, tn), lambda i, j: (0, jnp.minimum(j, n_conv - 1)))],
        out_specs=pl.BlockSpec((tm, tn), lambda i, j: (i, j)),
        out_shape=jax.ShapeDtypeStruct((T, N), BF16),
        scratch_shapes=[pltpu.VMEM((tm, D), BF16),
                        pltpu.VMEM((n_conv, SUBLANES, tn), F32)],
        compiler_params=_params(("arbitrary", "arbitrary")),
        name="gdn_inproj_conv",
    )(x, g.reshape(1, D), w, conv_w)


def _gate_kernel(ba_ref, alog_ref, dtb_ref, beta_ref, gc_ref):
    nh = beta_ref.shape[0]
    b_raw = ba_ref[:nh, :]
    a_raw = ba_ref[nh:2 * nh, :]
    beta_ref[...] = _sigmoid(b_raw)
    xs = a_raw + dtb_ref[...]
    softplus = jnp.maximum(xs, 0.0) + jnp.log(1.0 + jnp.exp(-jnp.abs(xs)))
    g = -jnp.exp(alog_ref[...]) * softplus
    pos = lax.broadcasted_iota(jnp.int32, g.shape, 1) % GDN_CHUNK
    s = 1
    while s < GDN_CHUNK:
        g = g + jnp.where(pos >= s, pltpu.roll(g, s, 1), 0.0)
        s *= 2
    gc_ref[...] = g


def gdn_gates(ba_t, a_log, dt_bias, *, tl=2048):
    nh2, T = ba_t.shape
    nh = nh2 // 2
    tl = min(tl, T)
    assert T % tl == 0 and tl % GDN_CHUNK == 0
    return pl.pallas_call(
        _gate_kernel,
        grid=(T // tl,),
        in_specs=[pl.BlockSpec((nh2, tl), lambda i: (0, i)),
                  pl.BlockSpec((nh, 1), lambda i: (0, 0)),
                  pl.BlockSpec((nh, 1), lambda i: (0, 0))],
        out_specs=[pl.BlockSpec((nh, tl), lambda i: (0, i)),
                   pl.BlockSpec((nh, tl), lambda i: (0, i))],
        out_shape=[jax.ShapeDtypeStruct((nh, T), F32), jax.ShapeDtypeStruct((nh, T), F32)],
        compiler_params=_params(("arbitrary",)),
        name="gdn_gates",
    )(ba_t, a_log.reshape(nh, 1), dt_bias.reshape(nh, 1))


def _bdot(a, b):
    return lax.dot_general(a, b, (((2,), (1,)), ((0,), (0,))), preferred_element_type=F32)


def _bdot_nt(a, b):
    return lax.dot_general(a, b, (((2,), (2,)), ((0,), (0,))), preferred_element_type=F32)


def _gdn_kernel(q_ref, k_ref, v_ref, z_ref, gcol_ref, bcol_ref, grow_ref, ong_ref, o_ref,
                s_ref, qw_ref, bn_ref, on_ref, egl_ref):
    p = pl.program_id(0)
    c = pl.program_id(1)
    C = GDN_CHUNK
    D = HEAD_DIM
    nc = bn_ref.shape[1]
    nh = s_ref.shape[0]
    npair = q_ref.shape[1] // D
    rep = nh // npair

    @pl.when(c == 0)
    def _():
        s_ref[...] = jnp.zeros_like(s_ref)

    ri = lax.broadcasted_iota(jnp.int32, (C, C), 0)
    ci = lax.broadcasted_iota(jnp.int32, (C, C), 1)
    causal = ri >= ci
    strict = (ri > ci)[None]
    eye = jnp.where(ri == ci, 1.0, 0.0)[None]
    lane = lax.broadcasted_iota(jnp.int32, gcol_ref.shape, 1)

    qscale = D ** -0.5
    for hp in range(nh):
        qp = hp // rep
        if hp % rep == 0:
            k16 = k_ref[:, qp * D:(qp + 1) * D].reshape(nc, C, D)
            q16 = q_ref[:, qp * D:(qp + 1) * D].reshape(nc, C, D)
            k3 = k16.astype(F32)
            q3 = q16.astype(F32) * qscale
            kk = _bdot_nt(k16, k16)
            qk = _bdot_nt(q16, k16) * qscale
        h = p * nh + hp
        gc3 = jnp.sum(jnp.where(lane == h, gcol_ref[...], 0.0), axis=1, keepdims=True).reshape(nc, C, 1)
        bt3 = jnp.sum(jnp.where(lane == h, bcol_ref[...], 0.0), axis=1, keepdims=True).reshape(nc, C, 1)
        decay = jnp.stack([
            jnp.where(causal, jnp.exp(jnp.where(causal, gc3[n] - grow_ref[hp, n:n + 1, :], 0.0)), 0.0)
            for n in range(nc)])
        lmat = jnp.where(strict, kk * bt3 * decay, 0.0)
        inv = eye - lmat
        m = lmat
        for _ in range(5):
            m16 = m.astype(BF16)
            m = _bdot(m16, m16)
            inv = inv + _bdot(inv.astype(BF16), m.astype(BF16))
        inv16 = inv.astype(BF16)
        eg3 = jnp.exp(gc3)
        v3 = v_ref[:, hp * D:(hp + 1) * D].reshape(nc, C, D).astype(F32)
        u16 = _bdot(inv16, (v3 * bt3).astype(BF16)).astype(BF16)
        w16 = _bdot(inv16, (k3 * (bt3 * eg3)).astype(BF16)).astype(BF16)
        attn16 = jnp.where(causal[None], qk * decay, 0.0).astype(BF16)
        gl3 = gc3[:, C - 1:C, :]
        kd3 = k3 * jnp.exp(gl3 - gc3)
        kdt16 = jnp.stack([kd3[n].T for n in range(nc)]).astype(BF16)
        qw_ref[hp, :, C:, :] = _bdot(kdt16, w16).astype(qw_ref.dtype)
        bn_ref[hp] = _bdot(kdt16, u16)
        qw_ref[hp, :, :C, :] = (q3 * eg3 - _bdot(attn16, w16)).astype(qw_ref.dtype)
        on_ref[hp] = _bdot(attn16, u16)
        egl_ref[hp] = jnp.broadcast_to(jnp.exp(gl3), egl_ref.shape[1:])

    ong = ong_ref[...]
    states = [s_ref[hp] for hp in range(nh)]
    for n in range(nc):
        sl = slice(n * C, (n + 1) * C)
        for hp in range(nh):
            s16 = states[hp].astype(BF16)
            qw_s = _dot(qw_ref[hp, n], s16)
            o = qw_s[:C] + on_ref[hp, n]
            states[hp] = states[hp] * egl_ref[hp, n, :1, :] - qw_s[C:] + bn_ref[hp, n]
            ms = jnp.mean(o * o, axis=-1, keepdims=True)
            on = o * lax.rsqrt(ms + RMS_EPS) * ong
            z = z_ref[sl, hp * D:(hp + 1) * D].astype(F32)
            o_ref[sl, hp * D:(hp + 1) * D] = (on * (z * _sigmoid(z))).astype(o_ref.dtype)
    for hp in range(nh):
        s_ref[hp] = states[hp]


def gdn_delta_rule(proj, gc_col, beta_col, gc_row, out_norm_g, *, tb=1024, npair=4):
    T = proj.shape[0]
    H = GDN_V_HEADS
    rep = GDN_V_HEADS // GDN_QK_HEADS
    kq = GDN_QK_HEADS
    C = GDN_CHUNK
    D = HEAD_DIM
    assert T % tb == 0 and tb % C == 0 and kq % npair == 0
    nc = tb // C
    nh = npair * rep
    k_blk0 = kq // npair
    v_blk0 = 2 * kq // nh
    z_blk0 = (2 * kq + H) // nh
    return pl.pallas_call(
        _gdn_kernel,
        grid=(kq // npair, T // tb),
        in_specs=[pl.BlockSpec((tb, npair * D), lambda p, c: (c, p)),
                  pl.BlockSpec((tb, npair * D), lambda p, c: (c, k_blk0 + p)),
                  pl.BlockSpec((tb, nh * D), lambda p, c: (c, v_blk0 + p)),
                  pl.BlockSpec((tb, nh * D), lambda p, c: (c, z_blk0 + p)),
                  pl.BlockSpec((tb, H), lambda p, c: (c, 0)),
                  pl.BlockSpec((tb, H), lambda p, c: (c, 0)),
                  pl.BlockSpec((nh, nc, C), lambda p, c: (p, c, 0)),
                  pl.BlockSpec((1, D), lambda p, c: (0, 0))],
        out_specs=pl.BlockSpec((tb, nh * D), lambda p, c: (c, p)),
        out_shape=jax.ShapeDtypeStruct((T, H * D), BF16),
        scratch_shapes=[pltpu.VMEM((nh, D, D), F32),
                        pltpu.VMEM((nh, nc, C + D, D), BF16),
                        pltpu.VMEM((nh, nc, D, D), F32),
                        pltpu.VMEM((nh, nc, C, D), F32),
                        pltpu.VMEM((nh, nc, SUBLANES, D), F32)],
        compiler_params=_params(("arbitrary", "arbitrary")),
        name="gdn_delta_rule",
    )(proj, proj, proj, proj, gc_col, beta_col, gc_row.reshape(H, T // C, C), out_norm_g.reshape(1, D))


def _cmp_kernel(x_ref, pos_ref, w1_ref, w2_ref, o_ref):
    x = x_ref[...]
    half = x.shape[1]
    top = (x + pos_ref[0:1, :]).astype(BF16)
    bot = (x + pos_ref[1:2, :]).astype(BF16)
    a = _dot(top, w1_ref[:half, :])
    b = _dot(bot, w1_ref[half:, :])
    pre = a + pltpu.roll(b, x.shape[0] - 1, 0)
    hid = pre * _sigmoid(pre)
    o_ref[...] = _dot(hid.astype(BF16), w2_ref[...]).astype(o_ref.dtype)


def compress_kv(kvc, pos, w1, w2):
    n8, nseg, half = kvc.shape
    G = n8 // 2
    hid = w1.shape[2]
    return pl.pallas_call(
        _cmp_kernel,
        grid=(n8,),
        in_specs=[pl.BlockSpec((None, nseg, half), lambda i: (i, 0, 0)),
                  pl.BlockSpec((2, half), lambda i: (0, 0)),
                  pl.BlockSpec((None, 2 * half, hid), lambda i: (i // G, 0, 0)),
                  pl.BlockSpec((None, hid, HEAD_DIM), lambda i: (i // G, 0, 0))],
        out_specs=pl.BlockSpec((None, nseg, HEAD_DIM), lambda i: (i, 0, 0)),
        out_shape=jax.ShapeDtypeStruct((n8, nseg, HEAD_DIM), BF16),
        compiler_params=_params(("arbitrary",)),
        name="nsa_compress",
    )(kvc, pos, w1, w2)


def _lane_reduce(x, op, reduce):
    n = x.shape[-1]
    if n % LANES == 0 and n > LANES:
        parts = [x[..., i * LANES:(i + 1) * LANES] for i in range(n // LANES)]
        while len(parts) > 1:
            parts = [op(parts[i], parts[i + 1]) if i + 1 < len(parts) else parts[i] for i in range(0, len(parts), 2)]
        x = parts[0]
    return reduce(x, axis=-1, keepdims=True)


def _row_max(x):
    return _lane_reduce(x, jnp.maximum, jnp.max)


def _row_sum(x):
    return _lane_reduce(x, jnp.add, jnp.sum)


def _ones_col(n):
    return jnp.where(lax.broadcasted_iota(jnp.int32, (n, LANES), 1) == 0, 1.0, 0.0).astype(BF16)


def _nsa_kernel(q_ref, gate_ref, slope_ref, saug_ref, kc_ref, vc_ref, ks_ref, vs_ref, kw_ref, vw_ref,
                wselt_ref, kaug_ref, o_ref, m_ref, l_ref, acc_ref, ocmp_ref, selt_ref, *, n_sel, tk, nqb):
    qb = pl.program_id(1)
    start = qb * Q_BLOCK
    HG = NSA_HPG
    R = HG * Q_BLOCK
    D = HEAD_DIM
    q = jnp.concatenate([q_ref[:, hh * D:(hh + 1) * D] for hh in range(HG)], axis=0)
    slope = slope_ref[...]
    tq = start + lax.broadcasted_iota(jnp.int32, (Q_BLOCK, 1), 0)
    tq_f = tq.astype(F32)

    ncb = kc_ref.shape[0]
    nsb = wselt_ref.shape[0]
    tq_row = start + lax.broadcasted_iota(jnp.int32, (1, Q_BLOCK), 1)

    def cmp_and_select(ncols, nrows):
        cidx = lax.broadcasted_iota(jnp.int32, (1, ncols), 1)
        centre = cidx.astype(F32) * CMP_STRIDE + (CMP_BLOCK - 1) / 2
        valid_c = ((cidx * CMP_STRIDE + (CMP_BLOCK - 1)) <= tq)[None]
        s_c = _dot_nt(q, kc_ref[:ncols, :]).reshape(HG, Q_BLOCK, ncols) - slope * (tq_f - centre)[None]
        s_c = jnp.where(valid_c, s_c, NEG_INF)
        e_c = jnp.where(valid_c, jnp.exp(s_c - _row_max(s_c)), 0.0)
        den_c = _row_sum(e_c)
        p_c = e_c * (1.0 / jnp.where(den_c > 0.0, den_c, 1.0))
        ocmp_ref[...] = _dot(p_c.reshape(R, ncols).astype(BF16), vc_ref[:ncols, :])
        imp = jnp.sum(p_c, axis=0)
        wselt = wselt_ref[:nrows, :ncols]
        i1 = imp.astype(BF16)
        r1 = imp - i1.astype(F32)
        i2 = r1.astype(BF16)
        i3 = (r1 - i2.astype(F32)).astype(BF16)
        imp_t = _dot_nt(wselt, i1) + (_dot_nt(wselt, i2) + _dot_nt(wselt, i3))
        blk = lax.broadcasted_iota(jnp.int32, (nrows, Q_BLOCK), 0)
        blk_f = blk.astype(F32)
        cur = tq_row // SEL_BLOCK
        forced = (blk == 0) | (blk == cur) | (blk == cur - 1)
        visible = blk * SEL_BLOCK <= tq_row
        score = jnp.where(visible, jnp.where(forced, FORCE_SCORE, imp_t), NEG_INF)
        picked = jnp.zeros((nrows, Q_BLOCK), F32)
        for _ in range(n_sel):
            mx = jnp.max(score, axis=0, keepdims=True)
            first = jnp.min(jnp.where(score == mx, blk_f, float(nrows)), axis=0, keepdims=True)
            pick = blk_f == first
            picked = jnp.where(pick, 1.0, picked)
            score = jnp.where(pick, -jnp.inf, score)
        selt_ref[:nrows, :] = picked
        if nrows < nsb:
            selt_ref[nrows:, :] = jnp.zeros((nsb - nrows, Q_BLOCK), F32)

    n_var = 4 if nqb % 4 == 0 else 1
    per_var = nqb // n_var
    for v in range(n_var):
        last_t = (v + 1) * per_var * Q_BLOCK - 1
        ncols = min(ncb, -(-(last_t // CMP_STRIDE) // LANES) * LANES)
        nrows = min(nsb, -(-(last_t // SEL_BLOCK + 1) // SUBLANES) * SUBLANES)

        @pl.when(qb // per_var == v)
        def _(ncols=ncols, nrows=nrows):
            cmp_and_select(ncols, nrows)

    o_cmp = ocmp_ref[...]
    sel_t = selt_ref[...]
    neg = (sel_t.T - 1.0) * (-NEG_INF)
    bpt = tk // SEL_BLOCK
    n_key_tiles = nsb // bpt
    tile_any = jnp.max(jnp.max(sel_t.reshape(n_key_tiles, bpt, Q_BLOCK), axis=2, keepdims=True), axis=1, keepdims=True)
    tile_bit = lax.shift_left(jnp.int32(1), lax.broadcasted_iota(jnp.int32, (n_key_tiles, 1, 1), 0))
    tile_bits = jnp.sum(jnp.where(tile_any > 0.0, tile_bit, 0))

    n_tiles = (start + Q_BLOCK - 1) // tk + 1
    kaug = kaug_ref[...]
    saug = saug_ref[...]
    lane_q = lax.broadcasted_iota(jnp.int32, (Q_BLOCK, LANES), 1)
    m_ref[...] = jnp.full(m_ref.shape, NEG_INF, F32)
    l_ref[...] = jnp.zeros(l_ref.shape, F32)
    acc_ref[...] = jnp.zeros(acc_ref.shape, F32)

    def sel_tile(kt, causal):
        k0 = pl.multiple_of(kt * tk, tk)
        shifted = pltpu.roll(neg, (nsb - kt * bpt) % nsb, 1)[:, :LANES]
        qa = jnp.concatenate([jnp.where(lane_q < bpt, shifted, saug[hh]).astype(BF16) for hh in range(HG)], axis=0)
        lhs = jnp.concatenate([q, qa], axis=1)
        rhs = jnp.concatenate([ks_ref[pl.ds(k0, tk), :], kaug], axis=1)
        v_one = jnp.concatenate([vs_ref[pl.ds(k0, tk), :], _ones_col(tk)], axis=1)
        rowoff = slope * (tq - k0).astype(F32)[None]
        s = _dot_nt(lhs, rhs).reshape(HG, Q_BLOCK, tk)
        if causal:
            kpos = k0 + lax.broadcasted_iota(jnp.int32, (1, tk), 1)
            s = jnp.where((kpos <= tq)[None], s, NEG_INF)
        m_run = m_ref[...]
        m_new = jnp.maximum(m_run, _row_max(s) - rowoff)
        p = jnp.exp((s - (m_new + rowoff)).astype(BF16))
        alpha = jnp.exp(m_run - m_new)
        pv = _dot(p.reshape(R, tk), v_one).reshape(HG, Q_BLOCK, 2 * LANES)
        l_ref[...] = alpha * l_ref[...] + pv[:, :, D:D + 1]
        acc_ref[...] = alpha * acc_ref[...] + pv[:, :, :D]
        m_ref[...] = m_new

    def sel_body(kt, carry):
        @pl.when((lax.shift_right_logical(tile_bits, kt) & 1) == 1)
        def _():
            sel_tile(kt, False)

        return carry

    lax.fori_loop(0, n_tiles - 1, sel_body, 0)
    sel_tile(n_tiles - 1, True)
    o_slc = acc_ref[...] / l_ref[...]

    wk = WINDOW + Q_BLOCK
    w0 = pl.multiple_of(jnp.maximum(start - WINDOW, 0), Q_BLOCK)
    kpos_w = w0 + lax.broadcasted_iota(jnp.int32, (1, wk), 1)
    dist_w = tq - kpos_w
    valid_w = ((dist_w >= 0) & (dist_w < WINDOW))[None]
    s_w = _dot_nt(q, kw_ref[pl.ds(w0, wk), :]).reshape(HG, Q_BLOCK, wk) - slope * dist_w.astype(F32)[None]
    s_w = jnp.where(valid_w, s_w, NEG_INF)
    e_w = jnp.exp(s_w - _row_max(s_w))
    den_w = _row_sum(e_w).reshape(R, 1)
    o_win = _dot(e_w.reshape(R, wk).astype(BF16), vw_ref[pl.ds(w0, wk), :]) / den_w

    gates = _sigmoid(gate_ref[...])
    for hh in range(HG):
        rows = slice(hh * Q_BLOCK, (hh + 1) * Q_BLOCK)
        out = (gates[:, hh:hh + 1] * o_cmp[rows]
               + gates[:, HG + hh:HG + hh + 1] * o_slc[hh]
               + gates[:, 2 * HG + hh:2 * HG + hh + 1] * o_win[rows])
        o_ref[:, hh * D:(hh + 1) * D] = out.astype(o_ref.dtype)


def nsa_attention(q, gate_logits, slopes, kv_cmp, kv_rest, *, tk=512):
    T = q.shape[0]
    G, HG, D = NSA_GROUPS, NSA_HPG, HEAD_DIM
    nqb = T // Q_BLOCK
    ncb = kv_cmp.shape[1]
    nsb = T // SEL_BLOCK
    n_sel = min(SEL_COUNT, nsb)
    bpt = tk // SEL_BLOCK
    assert T % tk == 0 and tk % SEL_BLOCK == 0 and T >= WINDOW + Q_BLOCK and bpt == SUBLANES
    nsb_pad = -(-nsb // LANES) * LANES
    assert nsb_pad // bpt <= 32
    jj = jnp.arange(nsb_pad)[:, None]
    cc = jnp.arange(ncb)[None, :]
    off = cc - (SEL_RATIO * jj - 1)
    wselt = jnp.zeros((nsb_pad, ncb), F32)
    for o, wt in enumerate(SEL_OVERLAP_W):
        wselt = jnp.where((off == o) & (jj < nsb), wt, wselt)
    wselt = wselt.astype(BF16)
    c = jnp.arange(tk)[:, None]
    lane = jnp.arange(LANES)[None, :]
    kaug = jnp.where(lane < bpt, (c // SEL_BLOCK == lane).astype(F32),
                     jnp.where(lane < bpt + 3, (c // SEL_BLOCK * SEL_BLOCK).astype(F32),
                               jnp.where(lane < bpt + 6, (c % SEL_BLOCK).astype(F32), 0.0))).astype(BF16)
    s1 = slopes.astype(BF16).astype(F32)
    s2 = (slopes - s1).astype(BF16).astype(F32)
    s3 = (slopes - s1 - s2).astype(BF16).astype(F32)
    pieces = jnp.stack([s1, s2, s3, s1, s2, s3], axis=-1)
    saug = jnp.pad(pieces, ((0, 0), (bpt, LANES - bpt - 6))).reshape(G, HG, 1, LANES)
    full = lambda rows, n: pl.BlockSpec((None, rows, D), lambda g, b: (n * G + g, 0, 0))
    return pl.pallas_call(
        functools.partial(_nsa_kernel, n_sel=n_sel, tk=tk, nqb=nqb),
        grid=(G, nqb),
        in_specs=[pl.BlockSpec((Q_BLOCK, HG * D), lambda g, b: (b, g)),
                  pl.BlockSpec((None, Q_BLOCK, 3 * HG), lambda g, b: (g, b, 0)),
                  pl.BlockSpec((None, HG, 1, 1), lambda g, b: (g, 0, 0, 0)),
                  pl.BlockSpec((None, HG, 1, LANES), lambda g, b: (g, 0, 0, 0)),
                  full(ncb, 0), full(ncb, 1), full(T, 0), full(T, 1), full(T, 2), full(T, 3),
                  pl.BlockSpec((nsb_pad, ncb), lambda g, b: (0, 0)),
                  pl.BlockSpec((tk, LANES), lambda g, b: (0, 0))],
        out_specs=pl.BlockSpec((Q_BLOCK, HG * D), lambda g, b: (b, g)),
        out_shape=jax.ShapeDtypeStruct((T, NSA_HEADS * D), BF16),
        scratch_shapes=[pltpu.VMEM((HG, Q_BLOCK, 1), F32),
                        pltpu.VMEM((HG, Q_BLOCK, 1), F32),
                        pltpu.VMEM((HG, Q_BLOCK, D), F32),
                        pltpu.VMEM((HG * Q_BLOCK, D), F32),
                        pltpu.VMEM((nsb_pad, Q_BLOCK), F32)],
        compiler_params=_params(("arbitrary", "arbitrary")),
        name="nsa_attention",
    )(q, gate_logits, slopes.reshape(G, HG, 1, 1), saug, kv_cmp, kv_cmp, kv_rest, kv_rest, kv_rest, kv_rest,
      wselt, kaug)


def _pad_cols(w, n):
    return jnp.pad(w, ((0, 0), (0, n - w.shape[1])))


def kernel(x, mixer_norm_g, ffn_norm_g, ffn_w_gate_up, ffn_w_down, gdn_w_in, gdn_conv_w, gdn_a_log, gdn_dt_bias, gdn_out_norm_g, gdn_w_out, kv_norm_g, kv_w, cmp_pos, cmp_w1_k, cmp_w2_k, cmp_w1_v, cmp_w2_v, nsa_w_in, nsa_w_out, final_norm_g):
    B, T, Dm = x.shape
    assert B == 1
    h = x.reshape(T, Dm)
    D = HEAD_DIM
    qk_dim = GDN_QK_HEADS * D
    v_dim = GDN_V_HEADS * D
    conv_dim = 2 * qk_dim + v_dim
    main = conv_dim + v_dim

    w_in = gdn_w_in[0]
    proj = gdn_inproj_conv(h, mixer_norm_g[0], w_in[:, :main].astype(BF16), gdn_conv_w[0],
                           n_qk_cols=2 * qk_dim)
    ba = norm_matmul(h, mixer_norm_g[0], _pad_cols(w_in[:, main:], LANES).astype(BF16), out_dtype=F32)
    beta_t, gc_t = gdn_gates(ba[:, :2 * GDN_V_HEADS].T, gdn_a_log[0], gdn_dt_bias[0])
    o = gdn_delta_rule(proj, gc_t.T, beta_t.T, gc_t, gdn_out_norm_g[0])
    h = matmul_residual(o, gdn_w_out[0].astype(BF16), h)
    act = norm_swiglu(h, ffn_norm_g[0], ffn_w_gate_up[0].astype(BF16))
    h = matmul_residual(act, ffn_w_down[0].astype(BF16), h)

    G = NSA_GROUPS
    n_cmp_cols = 2 * G * D
    kv_w16 = kv_w.astype(BF16)
    kvc = norm_matmul(h, kv_norm_g, kv_w16[:, :n_cmp_cols], out_dtype=F32, split=True)
    kvr = norm_matmul(h, kv_norm_g, kv_w16[:, n_cmp_cols:], out_dtype=BF16, split=True)
    nseg = T // CMP_STRIDE
    pos2 = cmp_pos.reshape(2, CMP_STRIDE * D)
    w1 = jnp.stack([cmp_w1_k, cmp_w1_v]).astype(BF16)
    w2 = jnp.stack([cmp_w2_k, cmp_w2_v]).astype(BF16)
    kv_cmp = compress_kv(kvc.reshape(2 * G, nseg, CMP_STRIDE * D), pos2, w1, w2)

    w_nsa = nsa_w_in[0]
    q_dim = NSA_HEADS * D
    q = norm_matmul(h, mixer_norm_g[1], w_nsa[:, :q_dim].astype(BF16), out_dtype=BF16, scale=D ** -0.5)
    gl = norm_matmul(h, mixer_norm_g[1], _pad_cols(w_nsa[:, q_dim:], LANES).astype(BF16), out_dtype=F32)
    gl = gl[:, :3 * NSA_HEADS].reshape(T, 3, G, NSA_HPG).transpose(2, 0, 1, 3).reshape(G, T, 3 * NSA_HPG)
    slopes = 2.0 ** (-8.0 * jnp.arange(1, NSA_HEADS + 1, dtype=F32) / NSA_HEADS)
    o = nsa_attention(q, gl, slopes, kv_cmp, kvr)
    h = matmul_residual(o, nsa_w_out[0].astype(BF16), h)
    act = norm_swiglu(h, ffn_norm_g[1], ffn_w_gate_up[1].astype(BF16))
    h = matmul_residual(act, ffn_w_down[1].astype(BF16), h)
    return rmsnorm(h, final_norm_g).reshape(B, T, Dm)
```
